```python
import math
import jax, jax.numpy as jnp
from jax import lax
import numpy as np

D_MODEL = 2048
BATCH = 1
SEQ = 16384
DEPTH = 2

HEAD_DIM = 128
BLOCK = 128
A_HEADS = 8
A_PATTERNS = ((128, 1), (512, 4), (2048, 16))
B_HEADS = 8
IDX_HEADS = 16
IDX_DIM = 64
B_TOPK_MAX = 256
C_HEADS = 8
C_QK_DIM = 64
D_HEADS = 8
Q_LORA = 768
KV_LORA = 512
NOPE_DIM = 128
ROPE_DIM = 64
V_DIM = 128
ROPE_THETA = 10000.0
D_FF = 5632
LN_EPS = 1e-5
RMS_EPS = 1e-6
DIFF_EPS = 1e-5
ALPHA = (2 * DEPTH) ** 0.25
BETA = (8 * DEPTH) ** -0.25
N_EVEN = (DEPTH + 1) // 2
N_ODD = DEPTH // 2
EVEN_SPLITS = (A_HEADS * HEAD_DIM, A_HEADS * HEAD_DIM, A_HEADS * HEAD_DIM,
               B_HEADS * HEAD_DIM, B_HEADS * HEAD_DIM, B_HEADS * HEAD_DIM,
               IDX_HEADS * IDX_DIM, IDX_DIM, IDX_HEADS)
ODD_SPLITS = (C_HEADS * 2 * C_QK_DIM, C_HEADS * 2 * C_QK_DIM, C_HEADS * 2 * C_QK_DIM,
              Q_LORA, KV_LORA, ROPE_DIM)
EVEN_IN = sum(EVEN_SPLITS)
ODD_IN = sum(ODD_SPLITS)
EVEN_OUT = A_HEADS * HEAD_DIM + B_HEADS * HEAD_DIM
ODD_OUT = C_HEADS * 2 * C_QK_DIM + D_HEADS * V_DIM

kernel_name = "hybrid_dilated_dsa_diff_mla_block"


def _layernorm(x, g, b):
    xf = x.astype(jnp.float32)
    mu = jnp.mean(xf, -1, keepdims=True)
    var = jnp.mean(jnp.square(xf - mu), -1, keepdims=True)
    return ((xf - mu) * lax.rsqrt(var + LN_EPS) * g + b).astype(x.dtype)


def _rmsnorm(x, g, eps):
    xf = x.astype(jnp.float32)
    return (xf * lax.rsqrt(jnp.mean(xf * xf, -1, keepdims=True) + eps) * g).astype(x.dtype)


def _swiglu(x, wg, wu, wd):
    return (jax.nn.silu(x @ wg) * (x @ wu)) @ wd


def _alibi_slopes(n):
    return jnp.asarray([2.0 ** (-8.0 * (h + 1) / n) for h in range(n)], jnp.float32)


def _split(h, widths):
    out, off = [], 0
    for w in widths:
        out.append(h[..., off:off + w])
        off += w
    return out


def _to_blocks(t):
    b, s = t.shape[:2]
    return jnp.moveaxis(t.reshape((b, s // BLOCK, BLOCK) + t.shape[2:]), 1, 0)


def _from_blocks(t):
    t = jnp.moveaxis(t, 0, 1)
    return t.reshape((t.shape[0], t.shape[1] * t.shape[2]) + t.shape[3:])


def _rope(x, pos):
    half = x.shape[-1] // 2
    inv = ROPE_THETA ** (-jnp.arange(half, dtype=jnp.float32) / half)
    ang = pos.astype(jnp.float32)[:, None] * inv[None, :]
    shape = (1, x.shape[1]) + (1,) * (x.ndim - 3) + (half,)
    cos, sin = jnp.cos(ang).reshape(shape), jnp.sin(ang).reshape(shape)
    xf = x.astype(jnp.float32)
    x1, x2 = xf[..., :half], xf[..., half:]
    return jnp.concatenate([x1 * cos - x2 * sin, x1 * sin + x2 * cos], -1).astype(x.dtype)


def _dilated_branch(q, k, v, slopes, window, dilation):
    b, s, h, dh = q.shape
    span = window // dilation
    unit = dilation * span
    sp = -(-s // unit) * unit
    nb = sp // unit
    pad = ((0, 0), (0, sp - s), (0, 0), (0, 0))

    def grp(t):
        return jnp.pad(t, pad).reshape(b, nb, span, dilation, h, dh)

    def band(t):
        prev = jnp.pad(t, ((0, 0), (1, 0), (0, 0), (0, 0), (0, 0), (0, 0)))[:, :-1]
        return jnp.concatenate([prev, t], axis=2)

    qg = grp(q)
    kb, vb = band(grp(k)), band(grp(v))
    sc = jnp.einsum('bnqchd,bnkchd->bnchqk', qg, kb).astype(jnp.float32) * (dh ** -0.5)
    qi = jnp.arange(span)[:, None]
    kj = jnp.arange(2 * span)[None, :]
    dist = qi - kj + span
    blk = jnp.arange(nb)[:, None, None]
    valid = (dist >= 0) & (dist <= span) & (blk * span + kj[None] - span >= 0)
    bias = -slopes[:, None, None] * (dist * dilation).astype(jnp.float32)[None]
    sc = jnp.where(valid[None, :, None, None], sc + bias, -jnp.inf)
    m = jnp.max(sc, -1, keepdims=True)
    p = jnp.exp(sc - m)
    l = jnp.sum(p, -1, keepdims=True)
    o = jnp.einsum('bnchqk,bnkchd->bnqchd', (p / l).astype(v.dtype), vb)
    lse = (m + jnp.log(l))[..., 0]
    o = o.reshape(b, sp, h, dh)[:, :s]
    lse = jnp.transpose(lse, (0, 1, 4, 2, 3)).reshape(b, sp, h)[:, :s]
    return o, lse


def _mixer_a(q, k, v, slopes):
    outs, lses = [], []
    for window, dilation in A_PATTERNS:
        o, lse = _dilated_branch(q, k, v, slopes, window, dilation)
        outs.append(o)
        lses.append(lse)
    wts = jax.nn.softmax(jnp.stack(lses), axis=0)
    o = jnp.einsum('gbsh,gbshd->bshd', wts, jnp.stack(outs).astype(jnp.float32))
    return o.astype(q.dtype)


def _mixer_b(q, k, v, qi, ki, wi, slopes):
    b, s, h, dh = q.shape
    topk = min(B_TOPK_MAX, s // 4)
    kpos = jnp.arange(s, dtype=jnp.int32)

    def block(args):
        qb, qib, wib, start = args
        qpos = start + jnp.arange(BLOCK, dtype=jnp.int32)
        isc = jnp.einsum('bqhd,bsd->bqhs', qib, ki).astype(jnp.float32)
        isc = jnp.einsum('bqhs,bqh->bqs', jax.nn.relu(isc), wib.astype(jnp.float32))
        isc = jnp.where((kpos[None, :] <= qpos[:, None])[None], isc, -jnp.inf)
        _, sel = lax.top_k(isc, topk)
        kg = jax.vmap(lambda t, i: t[i])(k, sel)
        vg = jax.vmap(lambda t, i: t[i])(v, sel)
        dist = qpos[None, :, None] - sel
        sc = jnp.einsum('bqhd,bqkhd->bhqk', qb, kg).astype(jnp.float32) * (dh ** -0.5)
        sc = sc - slopes[None, :, None, None] * dist[:, None].astype(jnp.float32)
        sc = jnp.where((dist >= 0)[:, None], sc, -jnp.inf)
        p = jax.nn.softmax(sc, axis=-1)
        return jnp.einsum('bhqk,bqkhd->bqhd', p.astype(v.dtype), vg)

    starts = jnp.arange(s // BLOCK, dtype=jnp.int32) * BLOCK
    return _from_blocks(lax.map(block, (_to_blocks(q), _to_blocks(qi), _to_blocks(wi), starts)))


def _even_mixer(x, w_in, w_out):
    b, s, _ = x.shape
    aq, ak, av, bq, bk, bv, iq, ik, iw = _split(x @ w_in, EVEN_SPLITS)
    hd = lambda t, n, d: t.reshape(b, s, n, d)
    oa = _mixer_a(hd(aq, A_HEADS, HEAD_DIM), hd(ak, A_HEADS, HEAD_DIM), hd(av, A_HEADS, HEAD_DIM),
                  _alibi_slopes(A_HEADS))
    ob = _mixer_b(hd(bq, B_HEADS, HEAD_DIM), hd(bk, B_HEADS, HEAD_DIM), hd(bv, B_HEADS, HEAD_DIM),
                  hd(iq, IDX_HEADS, IDX_DIM), ik, iw * ((IDX_HEADS * IDX_DIM) ** -0.5),
                  _alibi_slopes(B_HEADS))
    o = jnp.concatenate([oa.reshape(b, s, -1), ob.reshape(b, s, -1)], -1)
    return o @ w_out


def _mixer_c(q, k, v, lam_params, subln_g, lam_init, slopes):
    s = q.shape[1]
    lp = lam_params.astype(jnp.float32)
    lam = jnp.exp(jnp.sum(lp[0] * lp[1])) - jnp.exp(jnp.sum(lp[2] * lp[3])) + lam_init
    k1, k2 = k[:, :, :, 0], k[:, :, :, 1]
    kpos = jnp.arange(s, dtype=jnp.int32)
    scale = C_QK_DIM ** -0.5

    def block(args):
        qb, start = args
        qpos = start + jnp.arange(BLOCK, dtype=jnp.int32)
        dist = qpos[:, None] - kpos[None, :]
        bias = jnp.where(dist >= 0, -slopes[:, None, None] * dist.astype(jnp.float32)[None], -jnp.inf)
        s1 = jnp.einsum('bqhd,bshd->bhqs', qb[:, :, :, 0], k1).astype(jnp.float32) * scale + bias
        s2 = jnp.einsum('bqhd,bshd->bhqs', qb[:, :, :, 1], k2).astype(jnp.float32) * scale + bias
        attn = jax.nn.softmax(s1, axis=-1) - lam * jax.nn.softmax(s2, axis=-1)
        return jnp.einsum('bhqs,bshd->bqhd', attn.astype(v.dtype), v)

    starts = jnp.arange(s // BLOCK, dtype=jnp.int32) * BLOCK
    o = _from_blocks(lax.map(block, (_to_blocks(q), starts)))
    return _rmsnorm(o, subln_g, DIFF_EPS) * (1.0 - lam_init)


def _mixer_d(cq, ckv, kpe, gq, gkv, w_uq, w_ukv):
    b, s, _ = cq.shape
    pos = jnp.arange(s, dtype=jnp.int32)
    qf = (_rmsnorm(cq, gq, RMS_EPS) @ w_uq).reshape(b, s, D_HEADS, NOPE_DIM + ROPE_DIM)
    q_nope, q_pe = qf[..., :NOPE_DIM], _rope(qf[..., NOPE_DIM:], pos)
    kvf = (_rmsnorm(ckv, gkv, RMS_EPS) @ w_ukv).reshape(b, s, D_HEADS, NOPE_DIM + V_DIM)
    k_nope, v = kvf[..., :NOPE_DIM], kvf[..., NOPE_DIM:]
    k_pe = _rope(kpe, pos)
    scale = (NOPE_DIM + ROPE_DIM) ** -0.5

    def block(args):
        qnb, qpb, start = args
        qpos = start + jnp.arange(BLOCK, dtype=jnp.int32)
        sc = (jnp.einsum('bqhd,bshd->bhqs', qnb, k_nope)
              + jnp.einsum('bqhd,bsd->bhqs', qpb, k_pe)).astype(jnp.float32) * scale
        sc = jnp.where((pos[None, :] <= qpos[:, None])[None, None], sc, -jnp.inf)
        p = jax.nn.softmax(sc, axis=-1)
        return jnp.einsum('bhqs,bshd->bqhd', p.astype(v.dtype), v)

    starts = jnp.arange(s // BLOCK, dtype=jnp.int32) * BLOCK
    return _from_blocks(lax.map(block, (_to_blocks(q_nope), _to_blocks(q_pe), starts)))


def _odd_mixer(x, w_in, w_out, lam_params, subln_g, gq, gkv, w_uq, w_ukv, lam_init):
    b, s, _ = x.shape
    cq_, ck_, cv_, mq, mkv, mkpe = _split(x @ w_in, ODD_SPLITS)
    oc = _mixer_c(cq_.reshape(b, s, C_HEADS, 2, C_QK_DIM), ck_.reshape(b, s, C_HEADS, 2, C_QK_DIM),
                  cv_.reshape(b, s, C_HEADS, 2 * C_QK_DIM), lam_params, subln_g, lam_init,
                  _alibi_slopes(C_HEADS))
    od = _mixer_d(mq, mkv, mkpe, gq, gkv, w_uq, w_ukv)
    o = jnp.concatenate([oc.reshape(b, s, -1), od.reshape(b, s, -1)], -1)
    return o @ w_out


def setup_inputs(seed: int = 0) -> dict:
    key = jax.random.key(seed)
    ks = jax.random.split(key, 20)
    n = lambda k, shp: jax.random.normal(k, shp, jnp.float32)
    return {
        "x": n(ks[0], (BATCH, SEQ, D_MODEL)),
        "ffn_w_gate": n(ks[1], (DEPTH, 2, D_MODEL, D_FF)) * D_MODEL ** -0.5,
        "ffn_w_up": n(ks[2], (DEPTH, 2, D_MODEL, D_FF)) * D_MODEL ** -0.5,
        "ffn_w_down": n(ks[3], (DEPTH, 2, D_FF, D_MODEL)) * (D_FF ** -0.5 * BETA),
        "ln_g": 1.0 + 0.02 * n(ks[4], (DEPTH, 3, D_MODEL)),
        "ln_b": 0.02 * n(ks[5], (DEPTH, 3, D_MODEL)),
        "ev_w_in": n(ks[6], (N_EVEN, D_MODEL, EVEN_IN)) * D_MODEL ** -0.5,
        "ev_w_out": n(ks[7], (N_EVEN, EVEN_OUT, D_MODEL)) * (EVEN_OUT ** -0.5 * BETA),
        "od_w_in": n(ks[8], (N_ODD, D_MODEL, ODD_IN)) * D_MODEL ** -0.5,
        "od_w_out": n(ks[9], (N_ODD, ODD_OUT, D_MODEL)) * (ODD_OUT ** -0.5 * BETA),
        "od_lambda": 0.1 * n(ks[10], (N_ODD, 4, C_QK_DIM)),
        "od_subln_g": 1.0 + 0.02 * n(ks[11], (N_ODD, 2 * C_QK_DIM)),
        "od_q_norm_g": 1.0 + 0.02 * n(ks[12], (N_ODD, Q_LORA)),
        "od_kv_norm_g": 1.0 + 0.02 * n(ks[13], (N_ODD, KV_LORA)),
        "od_w_uq": n(ks[14], (N_ODD, Q_LORA, D_HEADS * (NOPE_DIM + ROPE_DIM))) * Q_LORA ** -0.5,
        "od_w_ukv": n(ks[15], (N_ODD, KV_LORA, D_HEADS * (NOPE_DIM + V_DIM))) * KV_LORA ** -0.5,
    }


def reference(x, ffn_w_gate, ffn_w_up, ffn_w_down, ln_g, ln_b, ev_w_in, ev_w_out,
              od_w_in, od_w_out, od_lambda, od_subln_g, od_q_norm_g, od_kv_norm_g,
              od_w_uq, od_w_ukv):
    for i in range(DEPTH):
        j = i // 2
        x = _layernorm(ALPHA * x + 0.5 * _swiglu(x, ffn_w_gate[i, 0], ffn_w_up[i, 0], ffn_w_down[i, 0]),
                       ln_g[i, 0], ln_b[i, 0])
        if i % 2 == 0:
            mix = _even_mixer(x, ev_w_in[j], ev_w_out[j])
        else:
            lam_init = 0.8 - 0.6 * math.exp(-0.3 * i)
            mix = _odd_mixer(x, od_w_in[j], od_w_out[j], od_lambda[j], od_subln_g[j],
                             od_q_norm_g[j], od_kv_norm_g[j], od_w_uq[j], od_w_ukv[j], lam_init)
        x = _layernorm(ALPHA * x + mix, ln_g[i, 1], ln_b[i, 1])
        x = _layernorm(ALPHA * x + 0.5 * _swiglu(x, ffn_w_gate[i, 1], ffn_w_up[i, 1], ffn_w_down[i, 1]),
                       ln_g[i, 2], ln_b[i, 2])
    return x
```

```python
import functools
import math

import numpy as np
import jax
import jax.numpy as jnp
from jax import lax
from jax.experimental import pallas as pl
from jax.experimental.pallas import tpu as pltpu

F32 = jnp.float32
BF16 = jnp.bfloat16

D_MODEL = 2048
DEPTH = 2
HEAD_DIM = 128
N_HEADS = 8
A_PATTERNS = ((128, 1), (512, 4), (2048, 16))
IDX_HEADS = 16
IDX_DIM = 64
B_TOPK_MAX = 256
C_QK_DIM = 64
Q_LORA = 768
KV_LORA = 512
NOPE_DIM = 128
ROPE_DIM = 64
V_DIM = 128
ROPE_THETA = 10000.0
D_FF = 5632
LN_EPS = 1e-5
RMS_EPS = 1e-6
DIFF_EPS = 1e-5
ALPHA = (2 * DEPTH) ** 0.25

LANE = 128
NEG = -1e30
VMEM_LIMIT = 56 * 1024 * 1024

HW = N_HEADS * HEAD_DIM


def _cparams(sem):
    return pltpu.CompilerParams(dimension_semantics=sem, vmem_limit_bytes=VMEM_LIMIT)


def _layernorm_rows(z, g, b):
    mu = jnp.mean(z, axis=-1, keepdims=True)
    zc = z - mu
    var = jnp.mean(zc * zc, axis=-1, keepdims=True)
    return zc * lax.rsqrt(var + LN_EPS) * g + b


def _ffn_ln_kernel(x_ref, wg_ref, wu_ref, wd_ref, g_ref, b_ref, y_ref, yb_ref, xb_sc, acc_sc):
    j = pl.program_id(1)

    @pl.when(j == 0)
    def _():
        xb_sc[...] = x_ref[...].astype(BF16)
        acc_sc[...] = jnp.zeros_like(acc_sc)

    xb = xb_sc[...]
    gate = jnp.dot(xb, wg_ref[...], preferred_element_type=F32)
    up = jnp.dot(xb, wu_ref[...], preferred_element_type=F32)
    h = (gate * jax.nn.sigmoid(gate)) * up
    acc_sc[...] += jnp.dot(h.astype(BF16), wd_ref[...], preferred_element_type=F32)

    @pl.when(j == pl.num_programs(1) - 1)
    def _():
        z = ALPHA * x_ref[...] + 0.5 * acc_sc[...]
        y = _layernorm_rows(z, g_ref[...], b_ref[...])
        y_ref[...] = y
        yb_ref[...] = y.astype(BF16)


def _ffn_ln(x, wg, wu, wd, g, b, *, tm=512, tf=512):
    s, d = x.shape
    dff = wg.shape[1]
    return pl.pallas_call(
        _ffn_ln_kernel,
        grid=(s // tm, dff // tf),
        in_specs=[
            pl.BlockSpec((tm, d), lambda i, j: (i, 0)),
            pl.BlockSpec((d, tf), lambda i, j: (0, j)),
            pl.BlockSpec((d, tf), lambda i, j: (0, j)),
            pl.BlockSpec((tf, d), lambda i, j: (j, 0)),
            pl.BlockSpec((1, d), lambda i, j: (0, 0)),
            pl.BlockSpec((1, d), lambda i, j: (0, 0)),
        ],
        out_specs=[
            pl.BlockSpec((tm, d), lambda i, j: (i, 0)),
            pl.BlockSpec((tm, d), lambda i, j: (i, 0)),
        ],
        out_shape=[jax.ShapeDtypeStruct((s, d), F32), jax.ShapeDtypeStruct((s, d), BF16)],
        scratch_shapes=[pltpu.VMEM((tm, d), BF16), pltpu.VMEM((tm, d), F32)],
        compiler_params=_cparams(("parallel", "arbitrary")),
        name="ffn_ln",
    )(x, wg, wu, wd, g.reshape(1, d), b.reshape(1, d))


def _mm_kernel(x_ref, w_ref, s_ref, o_ref):
    acc = jnp.dot(x_ref[...], w_ref[...], preferred_element_type=F32)
    o_ref[...] = (acc * s_ref[...]).astype(o_ref.dtype)


def _matmul(x, w, colscale, out_dtype, *, tm=512, tn=256, name="proj"):
    m, k = x.shape
    n = w.shape[1]
    return pl.pallas_call(
        _mm_kernel,
        grid=(m // tm, n // tn),
        in_specs=[
            pl.BlockSpec((tm, k), lambda i, j: (i, 0)),
            pl.BlockSpec((k, tn), lambda i, j: (0, j)),
            pl.BlockSpec((1, tn), lambda i, j: (0, j)),
        ],
        out_specs=pl.BlockSpec((tm, tn), lambda i, j: (i, j)),
        out_shape=jax.ShapeDtypeStruct((m, n), out_dtype),
        compiler_params=_cparams(("parallel", "arbitrary")),
        name=name,
    )(x, w, colscale.reshape(1, n))


def _proj_ln_kernel(oa_ref, ob_ref, wa_ref, wb_ref, x_ref, g_ref, b_ref, y_ref, yb_ref):
    mix = jnp.dot(oa_ref[...], wa_ref[...], preferred_element_type=F32)
    mix += jnp.dot(ob_ref[...], wb_ref[...], preferred_element_type=F32)
    y = _layernorm_rows(ALPHA * x_ref[...] + mix, g_ref[...], b_ref[...])
    y_ref[...] = y
    yb_ref[...] = y.astype(BF16)


def _proj_ln(oa, ob, wa, wb, x, g, b, *, tm=256):
    s, d = x.shape
    ka, kb = oa.shape[1], ob.shape[1]
    return pl.pallas_call(
        _proj_ln_kernel,
        grid=(s // tm,),
        in_specs=[
            pl.BlockSpec((tm, ka), lambda i: (i, 0)),
            pl.BlockSpec((tm, kb), lambda i: (i, 0)),
            pl.BlockSpec((ka, d), lambda i: (0, 0)),
            pl.BlockSpec((kb, d), lambda i: (0, 0)),
            pl.BlockSpec((tm, d), lambda i: (i, 0)),
            pl.BlockSpec((1, d), lambda i: (0, 0)),
            pl.BlockSpec((1, d), lambda i: (0, 0)),
        ],
        out_specs=[
            pl.BlockSpec((tm, d), lambda i: (i, 0)),
            pl.BlockSpec((tm, d), lambda i: (i, 0)),
        ],
        out_shape=[jax.ShapeDtypeStruct((s, d), F32), jax.ShapeDtypeStruct((s, d), BF16)],
        compiler_params=_cparams(("parallel",)),
        name="proj_ln",
    )(oa, ob, wa, wb, x, g.reshape(1, d), b.reshape(1, d))


_FIRST, _LAST, _DIAG = 1, 2, 4


def _attn_kernel(qi_ref, ki_ref, bi_ref, fl_ref, *refs, tq, tk, dq, slopes, bias_kind, diff,
                 lam_init):
    refs = list(refs)
    q_ref, k_ref, v_ref = refs[:3]
    pos = 3
    bias_ref = None
    if bias_kind is not None:
        bias_ref = refs[pos]
        pos += 1
    if diff:
        lam_ref, subg_ref = refs[pos], refs[pos + 1]
        pos += 2
    o_ref = refs[pos]
    m_sc, l_sc, acc_sc = refs[pos + 1:pos + 4]
    q2_sc = refs[pos + 4] if diff else None

    p = pl.program_id(0)
    fl = fl_ref[p]
    q0 = qi_ref[p] * tq
    k0 = ki_ref[p] * tk
    rows = 2 * tq if diff else tq

    @pl.when((fl & _FIRST) != 0)
    def _():
        m_sc[...] = jnp.full_like(m_sc, NEG)
        l_sc[...] = jnp.zeros_like(l_sc)
        acc_sc[...] = jnp.zeros_like(acc_sc)
        if diff:
            lane = lax.broadcasted_iota(jnp.int32, (tq, HEAD_DIM), 1)
            scale = jnp.asarray(C_QK_DIM ** -0.5, BF16)
            for h in range(N_HEADS):
                qh = q_ref[:, h * HEAD_DIM:(h + 1) * HEAD_DIM] * scale
                zero = jnp.zeros_like(qh)
                q2_sc[h, :tq, :] = jnp.where(lane < C_QK_DIM, qh, zero)
                q2_sc[h, tq:, :] = jnp.where(lane >= C_QK_DIM, qh, zero)

    def step(masked):
        if slopes is not None:
            kp = (k0 - q0).astype(F32) + lax.broadcasted_iota(jnp.int32, (1, tk), 1).astype(F32)
        if bias_ref is not None:
            bias = bias_ref[...].reshape(tq, tk).astype(F32)
        if masked:
            r = lax.broadcasted_iota(jnp.int32, (rows, tk), 0)
            if diff:
                r = jnp.where(r >= tq, r - tq, r)
            c = lax.broadcasted_iota(jnp.int32, (rows, tk), 1)
            keep = (r - c) >= (k0 - q0)
        for h in range(N_HEADS):
            if diff:
                q = q2_sc[h]
            else:
                q = q_ref[:, h * dq:(h + 1) * dq]
            k = k_ref[:, h * dq:(h + 1) * dq]
            s = lax.dot_general(q, k, (((1,), (1,)), ((), ())), preferred_element_type=F32)
            if slopes is not None:
                s = s + slopes[h] * kp
            if bias_ref is not None:
                s = s + bias
            if masked:
                s = jnp.where(keep, s, NEG)
            m_prev = m_sc[h]
            m_new = jnp.maximum(m_prev, jnp.max(s, axis=-1, keepdims=True))
            alpha = jnp.exp(m_prev - m_new)
            pexp = jnp.exp(s - m_new)
            l_sc[h] = alpha * l_sc[h] + jnp.sum(pexp, axis=-1, keepdims=True)
            pv = jnp.dot(pexp.astype(BF16), v_ref[:, h * HEAD_DIM:(h + 1) * HEAD_DIM],
                         preferred_element_type=F32)
            acc_sc[h] = alpha * acc_sc[h] + pv
            m_sc[h] = m_new

    if bias_kind is None:
        @pl.when((fl & _DIAG) != 0)
        def _():
            step(True)

        @pl.when((fl & _DIAG) == 0)
        def _():
            step(False)
    else:
        step(False)

    @pl.when((fl & _LAST) != 0)
    def _():
        if diff:
            lp = lam_ref[...]
            lam = (jnp.exp(jnp.sum(lp[0:1] * lp[1:2], axis=-1, keepdims=True))
                   - jnp.exp(jnp.sum(lp[2:3] * lp[3:4], axis=-1, keepdims=True)) + lam_init)
            for h in range(N_HEADS):
                o1 = acc_sc[h, :tq, :] / l_sc[h, :tq, :]
                o2 = acc_sc[h, tq:, :] / l_sc[h, tq:, :]
                o = o1 - lam * o2
                ms = jnp.mean(o * o, axis=-1, keepdims=True)
                o = o * lax.rsqrt(ms + DIFF_EPS) * subg_ref[...]
                o_ref[:, h * HEAD_DIM:(h + 1) * HEAD_DIM] = (o * (1.0 - lam_init)).astype(o_ref.dtype)
        else:
            for h in range(N_HEADS):
                o_ref[:, h * HEAD_DIM:(h + 1) * HEAD_DIM] = (acc_sc[h] / l_sc[h]).astype(o_ref.dtype)


def _causal_pairs(nq, lookback=None):
    qi, ki, bi, fl = [], [], [], []
    for q in range(nq):
        lo = 0 if lookback is None else max(0, q - lookback)
        for k in range(lo, q + 1):
            qi.append(q)
            ki.append(k)
            bi.append(q - k)
            fl.append((_FIRST if k == lo else 0) | (_LAST | _DIAG if k == q else 0))
    return tuple(np.asarray(a, np.int32) for a in (qi, ki, bi, fl))


def _attention(q_arr, k_arr, v_arr, *, q_col, k_col, v_col, dq, slopes=None, bias=None, bias_kind=None,
               lookback=None, diff=False, lam=None, subg=None, lam_init=0.0, tq=512, tk=512, name="attn"):
    s = q_arr.shape[0]
    assert tq == tk and s % tq == 0
    nq = s // tq
    qi, ki, bi, fl = _causal_pairs(nq, lookback)
    qw = N_HEADS * (HEAD_DIM if diff else dq)
    kw = N_HEADS * dq
    in_specs = [
        pl.BlockSpec((tq, qw), lambda p, qi, ki, bi, fl: (qi[p], q_col)),
        pl.BlockSpec((tk, kw), lambda p, qi, ki, bi, fl: (ki[p], k_col)),
        pl.BlockSpec((tk, HW), lambda p, qi, ki, bi, fl: (ki[p], v_col)),
    ]
    args = [q_arr, k_arr, v_arr]
    if bias_kind == "table":
        in_specs.append(pl.BlockSpec((1, tq, tk), lambda p, qi, ki, bi, fl: (bi[p], 0, 0)))
        args.append(bias)
    elif bias_kind == "mask":
        in_specs.append(pl.BlockSpec((tq, tk), lambda p, qi, ki, bi, fl: (qi[p], ki[p])))
        args.append(bias)
    rows = 2 * tq if diff else tq
    scratch = [pltpu.VMEM((N_HEADS, rows, 1), F32), pltpu.VMEM((N_HEADS, rows, 1), F32),
               pltpu.VMEM((N_HEADS, rows, HEAD_DIM), F32)]
    if diff:
        in_specs.append(pl.BlockSpec((4, C_QK_DIM), lambda p, qi, ki, bi, fl: (0, 0)))
        in_specs.append(pl.BlockSpec((1, HEAD_DIM), lambda p, qi, ki, bi, fl: (0, 0)))
        args += [lam, subg.reshape(1, HEAD_DIM)]
        scratch.append(pltpu.VMEM((N_HEADS, rows, HEAD_DIM), BF16))
    kern = functools.partial(_attn_kernel, tq=tq, tk=tk, dq=dq, slopes=slopes, bias_kind=bias_kind,
                             diff=diff, lam_init=lam_init)
    return pl.pallas_call(
        kern,
        grid_spec=pltpu.PrefetchScalarGridSpec(
            num_scalar_prefetch=4,
            grid=(len(qi),),
            in_specs=in_specs,
            out_specs=pl.BlockSpec((tq, HW), lambda p, qi, ki, bi, fl: (qi[p], 0)),
            scratch_shapes=scratch,
        ),
        out_shape=jax.ShapeDtypeStruct((s, HW), BF16),
        compiler_params=_cparams(("arbitrary",)),
        name=name,
    )(jnp.asarray(qi), jnp.asarray(ki), jnp.asarray(bi), jnp.asarray(fl), *args)


def _alibi_slopes(n):
    return tuple(2.0 ** (-8.0 * (h + 1) / n) for h in range(n))


def _dilated_bias_table(t):
    max_win = max(w for w, _ in A_PATTERNS)
    n_off = max_win // t + 1
    r = np.arange(t)[:, None]
    c = np.arange(t)[None, :]
    out = np.empty((n_off, t, t), np.float32)
    for o in range(n_off):
        dist = o * t + r - c
        mult = np.zeros((t, t), np.float64)
        for win, dil in A_PATTERNS:
            mult += (dist >= 0) & (dist <= win) & (dist % dil == 0)
        out[o] = np.where(mult > 0, np.log(np.maximum(mult, 1.0)), NEG)
    return out


def _sortable_key(x):
    bits = pltpu.bitcast(x, jnp.int32)
    return jnp.where(bits < 0, bits ^ jnp.int32(0x7FFFFFFF), bits)


def _indexer_kernel(q_ref, e1_ref, e2_ref, w_ref, o_ref, key_sc, *, tq, kc, topk, wscale):
    qb = pl.program_id(0)
    q0 = qb * tq
    s_total = o_ref.shape[1]
    n_chunks = (q0 + tq + kc - 1) // kc
    w = w_ref[...] * wscale
    row = lax.broadcasted_iota(jnp.int32, (tq, kc), 0)
    col = lax.broadcasted_iota(jnp.int32, (tq, kc), 1)
    rel = row - col

    def score_chunk(c, carry):
        k0 = pl.multiple_of(c * kc, kc)
        k1 = e1_ref[pl.ds(k0, kc), :]
        k2 = e2_ref[pl.ds(k0, kc), :]
        acc = jnp.zeros((tq, kc), F32)
        for j in range(IDX_HEADS // 2):
            qp = q_ref[:, j * LANE:(j + 1) * LANE]
            s_even = lax.dot_general(qp, k1, (((1,), (1,)), ((), ())), preferred_element_type=F32)
            s_odd = lax.dot_general(qp, k2, (((1,), (1,)), ((), ())), preferred_element_type=F32)
            acc = acc + jnp.maximum(s_even, 0.0) * w[:, 2 * j:2 * j + 1]
            acc = acc + jnp.maximum(s_odd, 0.0) * w[:, 2 * j + 1:2 * j + 2]
        acc = jnp.where(rel >= k0 - q0, acc, -jnp.inf)
        key_sc[:, pl.ds(k0, kc)] = _sortable_key(acc)
        return carry

    lax.fori_loop(0, n_chunks, score_chunk, 0)

    def count_ge(cand):
        def body(c, cnt):
            k0 = pl.multiple_of(c * kc, kc)
            return cnt + jnp.where(key_sc[:, pl.ds(k0, kc)] >= cand, 1, 0)
        cnt = lax.fori_loop(0, n_chunks, body, jnp.zeros((tq, kc), jnp.int32))
        return jnp.sum(cnt, axis=-1, keepdims=True)

    int_min = jnp.int32(-2 ** 31)
    zero = jnp.zeros((tq, 1), jnp.int32)
    thr = jnp.where(count_ge(zero) >= topk, zero, int_min)

    def bit_step(i, thr):
        cand = thr | lax.shift_left(jnp.int32(1), 30 - i)
        return jnp.where(count_ge(cand) >= topk, cand, thr)

    thr = lax.fori_loop(0, 31, bit_step, thr)

    def mask_chunk(c, carry):
        k0 = pl.multiple_of(c * kc, kc)
        sel = (key_sc[:, pl.ds(k0, kc)] >= thr) & (rel >= k0 - q0)
        o_ref[:, pl.ds(k0, kc)] = jnp.where(sel, 0.0, NEG).astype(o_ref.dtype)
        return carry

    lax.fori_loop(0, n_chunks, mask_chunk, 0)

    def fill_chunk(c, carry):
        k0 = pl.multiple_of(c * kc, kc)
        o_ref[:, pl.ds(k0, kc)] = jnp.full((tq, kc), NEG, o_ref.dtype)
        return carry

    lax.fori_loop(n_chunks, s_total // kc, fill_chunk, 0)


def _indexer_mask(proj, iw, *, q_col, e1_col, e2_col, topk, tq=128, kc=512):
    s = proj.shape[0]
    kern = functools.partial(_indexer_kernel, tq=tq, kc=kc, topk=topk,
                             wscale=(IDX_HEADS * IDX_DIM) ** -0.5)
    return pl.pallas_call(
        kern,
        grid=(s // tq,),
        in_specs=[
            pl.BlockSpec((tq, IDX_HEADS * IDX_DIM), lambda i: (i, q_col)),
            pl.BlockSpec((s, LANE), lambda i: (0, e1_col)),
            pl.BlockSpec((s, LANE), lambda i: (0, e2_col)),
            pl.BlockSpec((tq, LANE), lambda i: (i, 0)),
        ],
        out_specs=pl.BlockSpec((tq, s), lambda i: (i, 0)),
        out_shape=jax.ShapeDtypeStruct((s, s), BF16),
        scratch_shapes=[pltpu.VMEM((tq, s), jnp.int32)],
        compiler_params=_cparams(("parallel",)),
        name="indexer_topk",
    )(proj, proj, proj, iw)


def _rms_rows(x, g, eps):
    return x * lax.rsqrt(jnp.mean(x * x, axis=-1, keepdims=True) + eps) * g


def _mla_q_kernel(cq_ref, g_ref, wm_ref, wr_ref, cos_ref, sin_ref, o_ref, *, scale):
    n = _rms_rows(cq_ref[...], g_ref[...], RMS_EPS).astype(BF16)
    main = jnp.dot(n, wm_ref[...], preferred_element_type=F32)
    rot = jnp.dot(n, wr_ref[...], preferred_element_type=F32)
    cos, sin = cos_ref[...], sin_ref[...]
    for h in range(N_HEADS):
        base = h * 2 * LANE
        o_ref[:, base:base + LANE] = (main[:, base:base + LANE] * scale).astype(o_ref.dtype)
        pe = main[:, base + LANE:base + 2 * LANE] * cos + rot[:, h * LANE:(h + 1) * LANE] * sin
        o_ref[:, base + LANE:base + 2 * LANE] = (pe * scale).astype(o_ref.dtype)


def _mla_kv_kernel(ckv_ref, ef_ref, g_ref, w_ref, cos_ref, sin_ref, k_ref, v_ref):
    n = _rms_rows(ckv_ref[...], g_ref[...], RMS_EPS).astype(BF16)
    kv = jnp.dot(n, w_ref[...], preferred_element_type=F32)
    ef = ef_ref[...]
    kpe = (ef[:, :LANE] * cos_ref[...] + ef[:, LANE:] * sin_ref[...]).astype(k_ref.dtype)
    for h in range(N_HEADS):
        base = h * 2 * LANE
        k_ref[:, base:base + LANE] = kv[:, h * LANE:(h + 1) * LANE].astype(k_ref.dtype)
        k_ref[:, base + LANE:base + 2 * LANE] = kpe
    v_ref[...] = kv[:, HW:].astype(v_ref.dtype)


def _mla_prep(aux, gq, gkv, wq_main, wq_rot, wkv, cos_t, sin_t, *, tm=512):
    s = aux.shape[0]
    scale = (NOPE_DIM + ROPE_DIM) ** -0.5
    q = pl.pallas_call(
        functools.partial(_mla_q_kernel, scale=scale),
        grid=(s // tm,),
        in_specs=[
            pl.BlockSpec((tm, Q_LORA), lambda i: (i, 1)),
            pl.BlockSpec((1, Q_LORA), lambda i: (0, 0)),
            pl.BlockSpec(wq_main.shape, lambda i: (0, 0)),
            pl.BlockSpec(wq_rot.shape, lambda i: (0, 0)),
            pl.BlockSpec((tm, LANE), lambda i: (i, 0)),
            pl.BlockSpec((tm, LANE), lambda i: (i, 0)),
        ],
        out_specs=pl.BlockSpec((tm, 2 * HW), lambda i: (i, 0)),
        out_shape=jax.ShapeDtypeStruct((s, 2 * HW), BF16),
        compiler_params=_cparams(("parallel",)),
        name="mla_q_prep",
    )(aux, gq.reshape(1, Q_LORA), wq_main, wq_rot, cos_t, sin_t)
    k, v = pl.pallas_call(
        _mla_kv_kernel,
        grid=(s // tm,),
        in_specs=[
            pl.BlockSpec((tm, KV_LORA), lambda i: (i, 0)),
            pl.BlockSpec((tm, 2 * LANE), lambda i: (i, 2)),
            pl.BlockSpec((1, KV_LORA), lambda i: (0, 0)),
            pl.BlockSpec(wkv.shape, lambda i: (0, 0)),
            pl.BlockSpec((tm, LANE), lambda i: (i, 0)),
            pl.BlockSpec((tm, LANE), lambda i: (i, 0)),
        ],
        out_specs=[pl.BlockSpec((tm, 2 * HW), lambda i: (i, 0)), pl.BlockSpec((tm, HW), lambda i: (i, 0))],
        out_shape=[jax.ShapeDtypeStruct((s, 2 * HW), BF16), jax.ShapeDtypeStruct((s, HW), BF16)],
        compiler_params=_cparams(("parallel",)),
        name="mla_kv_prep",
    )(aux, aux, gkv.reshape(1, KV_LORA), wkv, cos_t, sin_t)
    return q, k, v


def _rope_tables(s):
    half = ROPE_DIM // 2
    inv = ROPE_THETA ** (-jnp.arange(half, dtype=F32) / half)
    ang = jnp.arange(s, dtype=jnp.int32).astype(F32)[:, None] * inv[None, :]
    cos, sin = jnp.cos(ang), jnp.sin(ang)
    pad = jnp.zeros((s, LANE - ROPE_DIM), F32)
    return (jnp.concatenate([cos, cos, pad], -1), jnp.concatenate([-sin, sin, pad], -1))


def _swap_halves(w):
    half = w.shape[-1] // 2
    return jnp.concatenate([w[..., half:], w[..., :half]], -1)


def _even_mixer(y, yb, w_in, w_out, g, b):
    s = y.shape[0]
    d = w_in.shape[0]
    att_scale = HEAD_DIM ** -0.5
    n_qkv = 6 * HW
    iq_w = w_in[:, n_qkv:n_qkv + IDX_HEADS * IDX_DIM]
    ik_w = w_in[:, n_qkv + IDX_HEADS * IDX_DIM:n_qkv + IDX_HEADS * IDX_DIM + IDX_DIM]
    iw_w = w_in[:, n_qkv + IDX_HEADS * IDX_DIM + IDX_DIM:]
    z64 = jnp.zeros((d, LANE - IDX_DIM), F32)
    w_main = jnp.concatenate([w_in[:, :n_qkv], iq_w, ik_w, z64, z64, ik_w], -1).astype(BF16)
    ones = jnp.ones((HW,), F32)
    colscale = jnp.concatenate([ones * att_scale, ones, ones, ones * att_scale, ones, ones, ones,
                                jnp.ones((2 * LANE,), F32)])
    proj = _matmul(yb, w_main, colscale, BF16, name="even_in_proj")
    w_iw = jnp.concatenate([iw_w, jnp.zeros((d, LANE - IDX_HEADS), F32)], -1).astype(BF16)
    iw = _matmul(yb, w_iw, jnp.ones((LANE,), F32), F32, tn=LANE, name="even_iw_proj")

    slopes = _alibi_slopes(N_HEADS)
    t = 512
    table = jnp.asarray(_dilated_bias_table(t))
    lookback = max(w for w, _ in A_PATTERNS) // t
    oa = _attention(proj, proj, proj, q_col=0, k_col=1, v_col=2, dq=HEAD_DIM, slopes=slopes,
                    bias=table, bias_kind="table", lookback=lookback, tq=t, tk=t, name="dilated_attn")

    topk = min(B_TOPK_MAX, s // 4)
    mask = _indexer_mask(proj, iw, q_col=6, e1_col=7 * HW // LANE, e2_col=7 * HW // LANE + 1, topk=topk)
    ob = _attention(proj, proj, proj, q_col=3, k_col=4, v_col=5, dq=HEAD_DIM, slopes=slopes,
                    bias=mask, bias_kind="mask", name="dsa_attn")
    wo = w_out.astype(BF16)
    return _proj_ln(oa, ob, wo[:HW], wo[HW:], y, g, b)


def _odd_mixer(y, yb, w_in, w_out, lam_params, subln_g, gq, gkv, w_uq, w_ukv, lam_init, g, b):
    s = y.shape[0]
    d = w_in.shape[0]
    w_c = w_in[:, :3 * HW].astype(BF16)
    proj = _matmul(yb, w_c, jnp.ones((3 * HW,), F32), BF16, name="odd_in_proj")
    mq_w = w_in[:, 3 * HW:3 * HW + Q_LORA]
    mkv_w = w_in[:, 3 * HW + Q_LORA:3 * HW + Q_LORA + KV_LORA]
    kpe_w = w_in[:, 3 * HW + Q_LORA + KV_LORA:]
    z64 = jnp.zeros((d, LANE - ROPE_DIM), F32)
    w_aux = jnp.concatenate([mkv_w, kpe_w, z64, _swap_halves(kpe_w), z64, mq_w], -1).astype(BF16)
    aux = _matmul(yb, w_aux, jnp.ones((w_aux.shape[1],), F32), F32, name="odd_aux_proj")

    wq = w_uq.reshape(Q_LORA, N_HEADS, NOPE_DIM + ROPE_DIM)
    zq = jnp.zeros((Q_LORA, N_HEADS, LANE - ROPE_DIM), F32)
    wq_main = jnp.concatenate([wq, zq], -1).reshape(Q_LORA, 2 * HW).astype(BF16)
    wq_rot = jnp.concatenate([_swap_halves(wq[..., NOPE_DIM:]), zq], -1).reshape(Q_LORA, HW).astype(BF16)
    wkv = w_ukv.reshape(KV_LORA, N_HEADS, NOPE_DIM + V_DIM)
    wkv = jnp.concatenate([wkv[..., :NOPE_DIM].reshape(KV_LORA, HW),
                           wkv[..., NOPE_DIM:].reshape(KV_LORA, HW)], -1).astype(BF16)
    cos_t, sin_t = _rope_tables(s)
    mq, mk, mv = _mla_prep(aux, gq, gkv, wq_main, wq_rot, wkv, cos_t, sin_t)

    oc = _attention(proj, proj, proj, q_col=0, k_col=1, v_col=2, dq=HEAD_DIM,
                    slopes=_alibi_slopes(N_HEADS), diff=True, lam=lam_params, subg=subln_g,
                    lam_init=lam_init, name="diff_attn")
    od = _attention(mq, mk, mv, q_col=0, k_col=0, v_col=0, dq=2 * LANE, name="mla_attn")
    wo = w_out.astype(BF16)
    return _proj_ln(oc, od, wo[:HW], wo[HW:], y, g, b)


def kernel(x, ffn_w_gate, ffn_w_up, ffn_w_down, ln_g, ln_b, ev_w_in, ev_w_out, od_w_in, od_w_out,
           od_lambda, od_subln_g, od_q_norm_g, od_kv_norm_g, od_w_uq, od_w_ukv):
    batch = x.shape[0]
    outs = []
    for bi in range(batch):
        y = x[bi]
        for i in range(DEPTH):
            j = i // 2
            y, yb = _ffn_ln(y, ffn_w_gate[i, 0].astype(BF16), ffn_w_up[i, 0].astype(BF16),
                            ffn_w_down[i, 0].astype(BF16), ln_g[i, 0], ln_b[i, 0])
            if i % 2 == 0:
                y, yb = _even_mixer(y, yb, ev_w_in[j], ev_w_out[j], ln_g[i, 1], ln_b[i, 1])
            else:
                lam_init = 0.8 - 0.6 * math.exp(-0.3 * i)
                y, yb = _odd_mixer(y, yb, od_w_in[j], od_w_out[j], od_lambda[j], od_subln_g[j],
                                   od_q_norm_g[j], od_kv_norm_g[j], od_w_uq[j], od_w_ukv[j], lam_init,
                                   ln_g[i, 1], ln_b[i, 1])
            y, yb = _ffn_ln(y, ffn_w_gate[i, 1].astype(BF16), ffn_w_up[i, 1].astype(BF16),
                            ffn_w_down[i, 1].astype(BF16), ln_g[i, 2], ln_b[i, 2])
        outs.append(y)
    return jnp.stack(outs)
```

```python
import functools
import math

import numpy as np
import jax
import jax.numpy as jnp
from jax import lax
from jax.experimental import pallas as pl
from jax.experimental.pallas import tpu as pltpu

F32 = jnp.float32
BF16 = jnp.bfloat16

D_MODEL = 2048
DEPTH = 2
HEAD_DIM = 128
N_HEADS = 8
A_PATTERNS = ((128, 1), (512, 4), (2048, 16))
IDX_HEADS = 16
IDX_DIM = 64
B_TOPK_MAX = 256
C_QK_DIM = 64
Q_LORA = 768
KV_LORA = 512
NOPE_DIM = 128
ROPE_DIM = 64
V_DIM = 128
ROPE_THETA = 10000.0
D_FF = 5632
LN_EPS = 1e-5
RMS_EPS = 1e-6
DIFF_EPS = 1e-5
ALPHA = (2 * DEPTH) ** 0.25

LANE = 128
NEG = -1e30
LOG2E = math.log2(math.e)
VMEM_LIMIT = 56 * 1024 * 1024

HW = N_HEADS * HEAD_DIM


def _cparams(sem):
    return pltpu.CompilerParams(dimension_semantics=sem, vmem_limit_bytes=VMEM_LIMIT)


def _layernorm_rows(z, g, b):
    mu = jnp.mean(z, axis=-1, keepdims=True)
    zc = z - mu
    var = jnp.mean(zc * zc, axis=-1, keepdims=True)
    return zc * lax.rsqrt(var + LN_EPS) * g + b


def _ffn_ln_kernel(x_ref, wg_ref, wu_ref, wd_ref, g_ref, b_ref, y_ref, yb_ref, xb_sc, acc_sc):
    j = pl.program_id(1)

    @pl.when(j == 0)
    def _():
        xb_sc[...] = x_ref[...].astype(BF16)
        acc_sc[...] = jnp.zeros_like(acc_sc)

    xb = xb_sc[...]
    gate = jnp.dot(xb, wg_ref[...], preferred_element_type=F32)
    up = jnp.dot(xb, wu_ref[...], preferred_element_type=F32)
    h = (gate * jax.nn.sigmoid(gate)) * up
    acc_sc[...] += jnp.dot(h.astype(BF16), wd_ref[...], preferred_element_type=F32)

    @pl.when(j == pl.num_programs(1) - 1)
    def _():
        z = ALPHA * x_ref[...] + 0.5 * acc_sc[...]
        y = _layernorm_rows(z, g_ref[...], b_ref[...])
        y_ref[...] = y
        yb_ref[...] = y.astype(BF16)


def _ffn_ln(x, wg, wu, wd, g, b, *, tm=512, tf=512):
    s, d = x.shape
    dff = wg.shape[1]
    return pl.pallas_call(
        _ffn_ln_kernel,
        grid=(s // tm, dff // tf),
        in_specs=[
            pl.BlockSpec((tm, d), lambda i, j: (i, 0)),
            pl.BlockSpec((d, tf), lambda i, j: (0, j)),
            pl.BlockSpec((d, tf), lambda i, j: (0, j)),
            pl.BlockSpec((tf, d), lambda i, j: (j, 0)),
            pl.BlockSpec((1, d), lambda i, j: (0, 0)),
            pl.BlockSpec((1, d), lambda i, j: (0, 0)),
        ],
        out_specs=[
            pl.BlockSpec((tm, d), lambda i, j: (i, 0)),
            pl.BlockSpec((tm, d), lambda i, j: (i, 0)),
        ],
        out_shape=[jax.ShapeDtypeStruct((s, d), F32), jax.ShapeDtypeStruct((s, d), BF16)],
        scratch_shapes=[pltpu.VMEM((tm, d), BF16), pltpu.VMEM((tm, d), F32)],
        compiler_params=_cparams(("parallel", "arbitrary")),
        name="ffn_ln",
    )(x, wg, wu, wd, g.reshape(1, d), b.reshape(1, d))


def _mm_kernel(x_ref, w_ref, s_ref, o_ref):
    acc = jnp.dot(x_ref[...], w_ref[...], preferred_element_type=F32)
    o_ref[...] = (acc * s_ref[...]).astype(o_ref.dtype)


def _matmul(x, w, colscale, out_dtype, *, tm=1024, tn=512, name="proj"):
    m, k = x.shape
    n = w.shape[1]
    tm, tn = min(tm, m), min(tn, n)
    assert m % tm == 0 and n % tn == 0
    return pl.pallas_call(
        _mm_kernel,
        grid=(m // tm, n // tn),
        in_specs=[
            pl.BlockSpec((tm, k), lambda i, j: (i, 0)),
            pl.BlockSpec((k, tn), lambda i, j: (0, j)),
            pl.BlockSpec((1, tn), lambda i, j: (0, j)),
        ],
        out_specs=pl.BlockSpec((tm, tn), lambda i, j: (i, j)),
        out_shape=jax.ShapeDtypeStruct((m, n), out_dtype),
        compiler_params=_cparams(("parallel", "arbitrary")),
        name=name,
    )(x, w, colscale.reshape(1, n))


def _proj_ln_kernel(oa_ref, ob_ref, wa_ref, wb_ref, x_ref, g_ref, b_ref, y_ref, yb_ref):
    mix = jnp.dot(oa_ref[...], wa_ref[...], preferred_element_type=F32)
    mix += jnp.dot(ob_ref[...], wb_ref[...], preferred_element_type=F32)
    y = _layernorm_rows(ALPHA * x_ref[...] + mix, g_ref[...], b_ref[...])
    y_ref[...] = y
    yb_ref[...] = y.astype(BF16)


def _proj_ln(oa, ob, wa, wb, x, g, b, *, tm=256):
    s, d = x.shape
    ka, kb = oa.shape[1], ob.shape[1]
    return pl.pallas_call(
        _proj_ln_kernel,
        grid=(s // tm,),
        in_specs=[
            pl.BlockSpec((tm, ka), lambda i: (i, 0)),
            pl.BlockSpec((tm, kb), lambda i: (i, 0)),
            pl.BlockSpec((ka, d), lambda i: (0, 0)),
            pl.BlockSpec((kb, d), lambda i: (0, 0)),
            pl.BlockSpec((tm, d), lambda i: (i, 0)),
            pl.BlockSpec((1, d), lambda i: (0, 0)),
            pl.BlockSpec((1, d), lambda i: (0, 0)),
        ],
        out_specs=[
            pl.BlockSpec((tm, d), lambda i: (i, 0)),
            pl.BlockSpec((tm, d), lambda i: (i, 0)),
        ],
        out_shape=[jax.ShapeDtypeStruct((s, d), F32), jax.ShapeDtypeStruct((s, d), BF16)],
        compiler_params=_cparams(("parallel",)),
        name="proj_ln",
    )(oa, ob, wa, wb, x, g.reshape(1, d), b.reshape(1, d))


_FIRST, _LAST, _DIAG = 1, 2, 4


def _attn_kernel(qi_ref, ki_ref, bi_ref, fl_ref, *refs, tq, tk, dq, slopes, bias_kind, diff,
                 lam_init):
    refs = list(refs)
    q_ref, k_ref, v_ref = refs[:3]
    pos = 3
    bias_ref = None
    if bias_kind is not None:
        bias_ref = refs[pos]
        pos += 1
    if diff:
        lam_ref, subg_ref = refs[pos], refs[pos + 1]
        pos += 2
    o_ref = refs[pos]
    m_sc, l_sc, acc_sc = refs[pos + 1:pos + 4]
    q2_sc = refs[pos + 4] if diff else None

    p = pl.program_id(0)
    fl = fl_ref[p]
    q0 = qi_ref[p] * tq
    k0 = ki_ref[p] * tk
    rows = 2 * tq if diff else tq

    @pl.when((fl & _FIRST) != 0)
    def _():
        m_sc[...] = jnp.full_like(m_sc, NEG)
        l_sc[...] = jnp.zeros_like(l_sc)
        acc_sc[...] = jnp.zeros_like(acc_sc)
        if diff:
            lane = lax.broadcasted_iota(jnp.int32, (tq, HEAD_DIM), 1)
            for h in range(N_HEADS):
                qh = q_ref[:, h * HEAD_DIM:(h + 1) * HEAD_DIM]
                zero = jnp.zeros_like(qh)
                q2_sc[h, :tq, :] = jnp.where(lane < C_QK_DIM, qh, zero)
                q2_sc[h, tq:, :] = jnp.where(lane >= C_QK_DIM, qh, zero)

    def step(masked):
        if slopes is not None:
            kp = (k0 - q0).astype(F32) + lax.broadcasted_iota(jnp.int32, (1, tk), 1).astype(F32)
        if bias_ref is not None:
            bias = bias_ref[...].reshape(tq, tk).astype(F32)
        if masked:
            r = lax.broadcasted_iota(jnp.int32, (rows, tk), 0)
            if diff:
                r = jnp.where(r >= tq, r - tq, r)
            c = lax.broadcasted_iota(jnp.int32, (rows, tk), 1)
            keep = (r - c) >= (k0 - q0)
        ones = jnp.ones((tk, LANE), BF16)
        for h in range(N_HEADS):
            if diff:
                q = q2_sc[h]
            else:
                q = q_ref[:, h * dq:(h + 1) * dq]
            k = k_ref[:, h * dq:(h + 1) * dq]
            s = lax.dot_general(q, k, (((1,), (1,)), ((), ())), preferred_element_type=F32)
            if slopes is not None:
                s = s + (slopes[h] * LOG2E) * kp
            if bias_ref is not None:
                s = s + bias
            if masked:
                s = jnp.where(keep, s, NEG)
            m_prev = m_sc[h]
            m_new = jnp.maximum(m_prev, jnp.max(s, axis=-1, keepdims=True))
            alpha = jnp.exp2(m_prev - m_new)
            pexp = jnp.exp2(s - jnp.concatenate([m_new] * (tk // LANE), axis=-1))
            v_aug = jnp.concatenate([v_ref[:, h * HEAD_DIM:(h + 1) * HEAD_DIM], ones], axis=-1)
            pv = jnp.dot(pexp.astype(BF16), v_aug, preferred_element_type=F32)
            l_sc[h] = alpha * l_sc[h] + pv[:, HEAD_DIM:]
            acc_sc[h] = alpha * acc_sc[h] + pv[:, :HEAD_DIM]
            m_sc[h] = m_new

    if bias_kind is None:
        @pl.when((fl & _DIAG) != 0)
        def _():
            step(True)

        @pl.when((fl & _DIAG) == 0)
        def _():
            step(False)
    else:
        step(False)

    @pl.when((fl & _LAST) != 0)
    def _():
        if diff:
            lp = lam_ref[...]
            lam = (jnp.exp(jnp.sum(lp[0:1] * lp[1:2], axis=-1, keepdims=True))
                   - jnp.exp(jnp.sum(lp[2:3] * lp[3:4], axis=-1, keepdims=True)) + lam_init)
            for h in range(N_HEADS):
                o1 = acc_sc[h, :tq, :] / l_sc[h, :tq, :]
                o2 = acc_sc[h, tq:, :] / l_sc[h, tq:, :]
                o = o1 - lam * o2
                ms = jnp.mean(o * o, axis=-1, keepdims=True)
                o = o * lax.rsqrt(ms + DIFF_EPS) * subg_ref[...]
                o_ref[:, h * HEAD_DIM:(h + 1) * HEAD_DIM] = (o * (1.0 - lam_init)).astype(o_ref.dtype)
        else:
            for h in range(N_HEADS):
                o_ref[:, h * HEAD_DIM:(h + 1) * HEAD_DIM] = (acc_sc[h] / l_sc[h]).astype(o_ref.dtype)


def _causal_pairs(nq, lookback=None):
    qi, ki, bi, fl = [], [], [], []
    for q in range(nq):
        lo = 0 if lookback is None else max(0, q - lookback)
        for k in range(lo, q + 1):
            qi.append(q)
            ki.append(k)
            bi.append(q - k)
            fl.append((_FIRST if k == lo else 0) | (_LAST | _DIAG if k == q else 0))
    return tuple(np.asarray(a, np.int32) for a in (qi, ki, bi, fl))


def _attention(q_arr, k_arr, v_arr, *, q_col, k_col, v_col, dq, slopes=None, bias=None, bias_kind=None,
               lookback=None, diff=False, lam=None, subg=None, lam_init=0.0, tq=512, tk=512, name="attn"):
    s = q_arr.shape[0]
    assert tq == tk and s % tq == 0
    nq = s // tq
    qi, ki, bi, fl = _causal_pairs(nq, lookback)
    qw = N_HEADS * (HEAD_DIM if diff else dq)
    kw = N_HEADS * dq
    in_specs = [
        pl.BlockSpec((tq, qw), lambda p, qi, ki, bi, fl: (qi[p], q_col)),
        pl.BlockSpec((tk, kw), lambda p, qi, ki, bi, fl: (ki[p], k_col)),
        pl.BlockSpec((tk, HW), lambda p, qi, ki, bi, fl: (ki[p], v_col)),
    ]
    args = [q_arr, k_arr, v_arr]
    if bias_kind == "table":
        in_specs.append(pl.BlockSpec((1, tq, tk), lambda p, qi, ki, bi, fl: (bi[p], 0, 0)))
        args.append(bias)
    elif bias_kind == "mask":
        in_specs.append(pl.BlockSpec((tq, tk), lambda p, qi, ki, bi, fl: (qi[p], ki[p])))
        args.append(bias)
    rows = 2 * tq if diff else tq
    scratch = [pltpu.VMEM((N_HEADS, rows, LANE), F32), pltpu.VMEM((N_HEADS, rows, LANE), F32),
               pltpu.VMEM((N_HEADS, rows, HEAD_DIM), F32)]
    if diff:
        in_specs.append(pl.BlockSpec((4, C_QK_DIM), lambda p, qi, ki, bi, fl: (0, 0)))
        in_specs.append(pl.BlockSpec((1, HEAD_DIM), lambda p, qi, ki, bi, fl: (0, 0)))
        args += [lam, subg.reshape(1, HEAD_DIM)]
        scratch.append(pltpu.VMEM((N_HEADS, rows, HEAD_DIM), BF16))
    kern = functools.partial(_attn_kernel, tq=tq, tk=tk, dq=dq, slopes=slopes, bias_kind=bias_kind,
                             diff=diff, lam_init=lam_init)
    return pl.pallas_call(
        kern,
        grid_spec=pltpu.PrefetchScalarGridSpec(
            num_scalar_prefetch=4,
            grid=(len(qi),),
            in_specs=in_specs,
            out_specs=pl.BlockSpec((tq, HW), lambda p, qi, ki, bi, fl: (qi[p], 0)),
            scratch_shapes=scratch,
        ),
        out_shape=jax.ShapeDtypeStruct((s, HW), BF16),
        compiler_params=_cparams(("arbitrary",)),
        name=name,
    )(jnp.asarray(qi), jnp.asarray(ki), jnp.asarray(bi), jnp.asarray(fl), *args)


def _alibi_slopes(n):
    return tuple(2.0 ** (-8.0 * (h + 1) / n) for h in range(n))


def _dilated_bias_table(t):
    max_win = max(w for w, _ in A_PATTERNS)
    n_off = max_win // t + 1
    r = np.arange(t)[:, None]
    c = np.arange(t)[None, :]
    out = np.empty((n_off, t, t), np.float32)
    for o in range(n_off):
        dist = o * t + r - c
        mult = np.zeros((t, t), np.float64)
        for win, dil in A_PATTERNS:
            mult += (dist >= 0) & (dist <= win) & (dist % dil == 0)
        out[o] = np.where(mult > 0, np.log2(np.maximum(mult, 1.0)), NEG)
    return out


def _sortable_key(x):
    bits = pltpu.bitcast(x, jnp.int32)
    return jnp.where(bits < 0, bits ^ jnp.int32(0x7FFFFFFF), bits)


_CAND = 16


def _bitonic_sort_desc(x):
    x = list(x)
    n = len(x)
    k = 2
    while k <= n:
        j = k // 2
        while j >= 1:
            for i in range(n):
                l = i ^ j
                if l > i:
                    hi, lo = jnp.maximum(x[i], x[l]), jnp.minimum(x[i], x[l])
                    x[i], x[l] = (hi, lo) if (i & k) == 0 else (lo, hi)
            j //= 2
        k *= 2
    return x


def _merge_top_desc(a, b):
    n = len(a)
    c = [jnp.maximum(a[i], b[n - 1 - i]) for i in range(n)]
    j = n // 2
    while j >= 1:
        for i in range(n):
            l = i ^ j
            if l > i:
                c[i], c[l] = jnp.maximum(c[i], c[l]), jnp.minimum(c[i], c[l])
        j //= 2
    return c


def _indexer_kernel(q_ref, e1_ref, e2_ref, w_ref, o_ref, key_sc, cand_sc, thr_sc, *, tq, kc, topk, wscale):
    qb = pl.program_id(0)
    q0 = qb * tq
    s_total = o_ref.shape[1]
    n_chunks = (q0 + tq + kc - 1) // kc
    w = w_ref[...] * wscale
    row = lax.broadcasted_iota(jnp.int32, (tq, kc), 0)
    col = lax.broadcasted_iota(jnp.int32, (tq, kc), 1)
    rel = row - col

    def score_chunk(c, carry):
        k0 = pl.multiple_of(c * kc, kc)
        k1 = e1_ref[pl.ds(k0, kc), :]
        k2 = e2_ref[pl.ds(k0, kc), :]
        acc = jnp.zeros((tq, kc), F32)
        for j in range(IDX_HEADS // 2):
            qp = q_ref[:, j * LANE:(j + 1) * LANE]
            s_even = lax.dot_general(qp, k1, (((1,), (1,)), ((), ())), preferred_element_type=F32)
            s_odd = lax.dot_general(qp, k2, (((1,), (1,)), ((), ())), preferred_element_type=F32)
            acc = acc + jnp.maximum(s_even, 0.0) * w[:, 2 * j:2 * j + 1]
            acc = acc + jnp.maximum(s_odd, 0.0) * w[:, 2 * j + 1:2 * j + 2]
        acc = jnp.where(acc == 0.0, 0.0, acc)
        key_sc[:, pl.ds(k0, kc)] = jnp.where(rel >= k0 - q0, acc, -jnp.inf)
        return carry

    lax.fori_loop(0, n_chunks, score_chunk, 0)

    int_min = jnp.int32(-2 ** 31)
    n_tiles = n_chunks * (kc // LANE)
    n_groups = (n_tiles + _CAND - 1) // _CAND

    def pad_tile(t, carry):
        key_sc[:, pl.ds(pl.multiple_of(t * LANE, LANE), LANE)] = jnp.full((tq, LANE), -jnp.inf, F32)
        return carry

    lax.fori_loop(n_tiles, n_groups * _CAND, pad_tile, 0)

    def row_group(r, carry):
        r0 = pl.multiple_of(r * 8, 8)

        def tile_group(g, top):
            base = g * (_CAND * LANE)
            x = [key_sc[pl.ds(r0, 8), pl.ds(pl.multiple_of(base + j * LANE, LANE), LANE)]
                 for j in range(_CAND)]
            return tuple(_merge_top_desc(list(top), _bitonic_sort_desc(x)))

        top0 = tuple(jnp.full((8, LANE), -jnp.inf, F32) for _ in range(_CAND))
        top = lax.fori_loop(0, n_groups, tile_group, top0)
        for j in range(_CAND):
            cand_sc[pl.ds(r0, 8), j * LANE:(j + 1) * LANE] = _sortable_key(top[j])
        return carry

    lax.fori_loop(0, tq // 8, row_group, 0)

    def count_cand(cand):
        cand_b = jnp.broadcast_to(cand, (tq, LANE))
        cnt = jnp.zeros((tq, LANE), jnp.int32)
        for j in range(_CAND):
            cnt = cnt + jnp.where(cand_sc[:, j * LANE:(j + 1) * LANE] >= cand_b, 1, 0)
        return jnp.sum(cnt, axis=-1, keepdims=True)

    def count_all(cand):
        cand_b = jnp.broadcast_to(cand, (tq, LANE))

        def body(t, cnt):
            blk = _sortable_key(key_sc[:, pl.ds(pl.multiple_of(t * LANE, LANE), LANE)])
            return cnt + jnp.where(blk >= cand_b, 1, 0)

        cnt = lax.fori_loop(0, n_tiles, body, jnp.zeros((tq, LANE), jnp.int32))
        return jnp.sum(cnt, axis=-1, keepdims=True)

    def kth_largest(count_ge):
        zero = jnp.zeros((tq, 1), jnp.int32)
        thr = jnp.where(count_ge(zero) >= topk, zero, int_min)

        def bit_step(i, thr):
            cand = thr | lax.shift_left(jnp.int32(1), 30 - i)
            return jnp.where(count_ge(cand) >= topk, cand, thr)

        return lax.fori_loop(0, 31, bit_step, thr)

    thr = kth_largest(count_cand)
    thr_sc[...] = jnp.broadcast_to(thr, (tq, LANE))
    smallest_kept = cand_sc[:, (_CAND - 1) * LANE:]
    unsafe = jnp.max(jnp.where(smallest_kept >= thr_sc[...], 1, 0)) > 0

    @pl.when(jnp.logical_and(n_groups > 1, unsafe))
    def _():
        thr_sc[...] = jnp.broadcast_to(kth_largest(count_all), (tq, LANE))

    thr_key = thr_sc[...]
    thr_b = pltpu.bitcast(jnp.where(thr_key < 0, thr_key ^ jnp.int32(0x7FFFFFFF), thr_key), F32)
    row_t = lax.broadcasted_iota(jnp.int32, (tq, LANE), 0)
    col_t = lax.broadcasted_iota(jnp.int32, (tq, LANE), 1)
    rel_t = row_t - col_t

    def mask_tile(t, carry):
        k0 = pl.multiple_of(t * LANE, LANE)
        sel = (key_sc[:, pl.ds(k0, LANE)] >= thr_b) & (rel_t >= k0 - q0)
        o_ref[:, pl.ds(k0, LANE)] = jnp.where(sel, 0.0, NEG).astype(o_ref.dtype)
        return carry

    lax.fori_loop(0, n_tiles, mask_tile, 0)

    def fill_chunk(c, carry):
        k0 = pl.multiple_of(c * kc, kc)
        o_ref[:, pl.ds(k0, kc)] = jnp.full((tq, kc), NEG, o_ref.dtype)
        return carry

    lax.fori_loop(n_chunks, s_total // kc, fill_chunk, 0)


def _indexer_mask(proj, iw, *, q_col, e1_col, e2_col, topk, tq=128, kc=512):
    s = proj.shape[0]
    kern = functools.partial(_indexer_kernel, tq=tq, kc=kc, topk=topk,
                             wscale=(IDX_HEADS * IDX_DIM) ** -0.5)
    return pl.pallas_call(
        kern,
        grid=(s // tq,),
        in_specs=[
            pl.BlockSpec((tq, IDX_HEADS * IDX_DIM), lambda i: (i, q_col)),
            pl.BlockSpec((s, LANE), lambda i: (0, e1_col)),
            pl.BlockSpec((s, LANE), lambda i: (0, e2_col)),
            pl.BlockSpec((tq, LANE), lambda i: (i, 0)),
        ],
        out_specs=pl.BlockSpec((tq, s), lambda i: (i, 0)),
        out_shape=jax.ShapeDtypeStruct((s, s), BF16),
        scratch_shapes=[pltpu.VMEM((tq, -(-s // (_CAND * LANE)) * _CAND * LANE), F32),
                        pltpu.VMEM((tq, _CAND * LANE), jnp.int32),
                        pltpu.VMEM((tq, LANE), jnp.int32)],
        compiler_params=_cparams(("parallel",)),
        name="indexer_topk",
    )(proj, proj, proj, iw)


def _rms_rows(x, g, eps):
    return x * lax.rsqrt(jnp.mean(x * x, axis=-1, keepdims=True) + eps) * g


def _mla_q_kernel(cq_ref, g_ref, wm_ref, wr_ref, cos_ref, sin_ref, o_ref, *, scale):
    n = _rms_rows(cq_ref[...], g_ref[...], RMS_EPS).astype(BF16)
    main = jnp.dot(n, wm_ref[...], preferred_element_type=F32)
    rot = jnp.dot(n, wr_ref[...], preferred_element_type=F32)
    cos, sin = cos_ref[...], sin_ref[...]
    for h in range(N_HEADS):
        base = h * 2 * LANE
        o_ref[:, base:base + LANE] = (main[:, base:base + LANE] * scale).astype(o_ref.dtype)
        pe = main[:, base + LANE:base + 2 * LANE] * cos + rot[:, h * LANE:(h + 1) * LANE] * sin
        o_ref[:, base + LANE:base + 2 * LANE] = (pe * scale).astype(o_ref.dtype)


def _mla_kv_kernel(ckv_ref, ef_ref, g_ref, w_ref, cos_ref, sin_ref, k_ref, v_ref):
    n = _rms_rows(ckv_ref[...], g_ref[...], RMS_EPS).astype(BF16)
    kv = jnp.dot(n, w_ref[...], preferred_element_type=F32)
    ef = ef_ref[...]
    kpe = (ef[:, :LANE] * cos_ref[...] + ef[:, LANE:] * sin_ref[...]).astype(k_ref.dtype)
    for h in range(N_HEADS):
        base = h * 2 * LANE
        k_ref[:, base:base + LANE] = kv[:, h * LANE:(h + 1) * LANE].astype(k_ref.dtype)
        k_ref[:, base + LANE:base + 2 * LANE] = kpe
    v_ref[...] = kv[:, HW:].astype(v_ref.dtype)


def _mla_prep(aux, gq, gkv, wq_main, wq_rot, wkv, cos_t, sin_t, *, tm=512):
    s = aux.shape[0]
    scale = (NOPE_DIM + ROPE_DIM) ** -0.5 * LOG2E
    q = pl.pallas_call(
        functools.partial(_mla_q_kernel, scale=scale),
        grid=(s // tm,),
        in_specs=[
            pl.BlockSpec((tm, Q_LORA), lambda i: (i, 1)),
            pl.BlockSpec((1, Q_LORA), lambda i: (0, 0)),
            pl.BlockSpec(wq_main.shape, lambda i: (0, 0)),
            pl.BlockSpec(wq_rot.shape, lambda i: (0, 0)),
            pl.BlockSpec((tm, LANE), lambda i: (i, 0)),
            pl.BlockSpec((tm, LANE), lambda i: (i, 0)),
        ],
        out_specs=pl.BlockSpec((tm, 2 * HW), lambda i: (i, 0)),
        out_shape=jax.ShapeDtypeStruct((s, 2 * HW), BF16),
        compiler_params=_cparams(("parallel",)),
        name="mla_q_prep",
    )(aux, gq.reshape(1, Q_LORA), wq_main, wq_rot, cos_t, sin_t)
    k, v = pl.pallas_call(
        _mla_kv_kernel,
        grid=(s // tm,),
        in_specs=[
            pl.BlockSpec((tm, KV_LORA), lambda i: (i, 0)),
            pl.BlockSpec((tm, 2 * LANE), lambda i: (i, 2)),
            pl.BlockSpec((1, KV_LORA), lambda i: (0, 0)),
            pl.BlockSpec(wkv.shape, lambda i: (0, 0)),
            pl.BlockSpec((tm, LANE), lambda i: (i, 0)),
            pl.BlockSpec((tm, LANE), lambda i: (i, 0)),
        ],
        out_specs=[pl.BlockSpec((tm, 2 * HW), lambda i: (i, 0)), pl.BlockSpec((tm, HW), lambda i: (i, 0))],
        out_shape=[jax.ShapeDtypeStruct((s, 2 * HW), BF16), jax.ShapeDtypeStruct((s, HW), BF16)],
        compiler_params=_cparams(("parallel",)),
        name="mla_kv_prep",
    )(aux, aux, gkv.reshape(1, KV_LORA), wkv, cos_t, sin_t)
    return q, k, v


def _rope_tables(s):
    half = ROPE_DIM // 2
    inv = ROPE_THETA ** (-jnp.arange(half, dtype=F32) / half)
    ang = jnp.arange(s, dtype=jnp.int32).astype(F32)[:, None] * inv[None, :]
    cos, sin = jnp.cos(ang), jnp.sin(ang)
    pad = jnp.zeros((s, LANE - ROPE_DIM), F32)
    return (jnp.concatenate([cos, cos, pad], -1), jnp.concatenate([-sin, sin, pad], -1))


def _swap_halves(w):
    half = w.shape[-1] // 2
    return jnp.concatenate([w[..., half:], w[..., :half]], -1)


def _even_mixer(y, yb, w_in, w_out, g, b):
    s = y.shape[0]
    d = w_in.shape[0]
    att_scale = HEAD_DIM ** -0.5 * LOG2E
    n_qkv = 6 * HW
    iq_w = w_in[:, n_qkv:n_qkv + IDX_HEADS * IDX_DIM]
    ik_w = w_in[:, n_qkv + IDX_HEADS * IDX_DIM:n_qkv + IDX_HEADS * IDX_DIM + IDX_DIM]
    iw_w = w_in[:, n_qkv + IDX_HEADS * IDX_DIM + IDX_DIM:]
    z64 = jnp.zeros((d, LANE - IDX_DIM), F32)
    w_main = jnp.concatenate([w_in[:, :n_qkv], iq_w, ik_w, z64, z64, ik_w, jnp.zeros((d, 2 * LANE), F32)],
                             -1).astype(BF16)
    ones = jnp.ones((HW,), F32)
    colscale = jnp.concatenate([ones * att_scale, ones, ones, ones * att_scale, ones, ones, ones,
                                jnp.ones((4 * LANE,), F32)])
    proj = _matmul(yb, w_main, colscale, BF16, name="even_in_proj")
    w_iw = jnp.concatenate([iw_w, jnp.zeros((d, LANE - IDX_HEADS), F32)], -1).astype(BF16)
    iw = _matmul(yb, w_iw, jnp.ones((LANE,), F32), F32, tn=LANE, name="even_iw_proj")

    slopes = _alibi_slopes(N_HEADS)
    t = 512
    table = jnp.asarray(_dilated_bias_table(t))
    lookback = max(w for w, _ in A_PATTERNS) // t
    oa = _attention(proj, proj, proj, q_col=0, k_col=1, v_col=2, dq=HEAD_DIM, slopes=slopes,
                    bias=table, bias_kind="table", lookback=lookback, tq=t, tk=t, name="dilated_attn")

    topk = min(B_TOPK_MAX, s // 4)
    mask = _indexer_mask(proj, iw, q_col=6, e1_col=7 * HW // LANE, e2_col=7 * HW // LANE + 1, topk=topk)
    ob = _attention(proj, proj, proj, q_col=3, k_col=4, v_col=5, dq=HEAD_DIM, slopes=slopes,
                    bias=mask, bias_kind="mask", name="dsa_attn")
    wo = w_out.astype(BF16)
    return _proj_ln(oa, ob, wo[:HW], wo[HW:], y, g, b)


def _odd_mixer(y, yb, w_in, w_out, lam_params, subln_g, gq, gkv, w_uq, w_ukv, lam_init, g, b):
    s = y.shape[0]
    d = w_in.shape[0]
    w_c = w_in[:, :3 * HW].astype(BF16)
    c_scale = jnp.concatenate([jnp.full((HW,), C_QK_DIM ** -0.5 * LOG2E, F32), jnp.ones((2 * HW,), F32)])
    proj = _matmul(yb, w_c, c_scale, BF16, name="odd_in_proj")
    mq_w = w_in[:, 3 * HW:3 * HW + Q_LORA]
    mkv_w = w_in[:, 3 * HW + Q_LORA:3 * HW + Q_LORA + KV_LORA]
    kpe_w = w_in[:, 3 * HW + Q_LORA + KV_LORA:]
    z64 = jnp.zeros((d, LANE - ROPE_DIM), F32)
    w_aux = jnp.concatenate([mkv_w, kpe_w, z64, _swap_halves(kpe_w), z64, mq_w], -1).astype(BF16)
    aux = _matmul(yb, w_aux, jnp.ones((w_aux.shape[1],), F32), F32, name="odd_aux_proj")

    wq = w_uq.reshape(Q_LORA, N_HEADS, NOPE_DIM + ROPE_DIM)
    zq = jnp.zeros((Q_LORA, N_HEADS, LANE - ROPE_DIM), F32)
    wq_main = jnp.concatenate([wq, zq], -1).reshape(Q_LORA, 2 * HW).astype(BF16)
    wq_rot = jnp.concatenate([_swap_halves(wq[..., NOPE_DIM:]), zq], -1).reshape(Q_LORA, HW).astype(BF16)
    wkv = w_ukv.reshape(KV_LORA, N_HEADS, NOPE_DIM + V_DIM)
    wkv = jnp.concatenate([wkv[..., :NOPE_DIM].reshape(KV_LORA, HW),
                           wkv[..., NOPE_DIM:].reshape(KV_LORA, HW)], -1).astype(BF16)
    cos_t, sin_t = _rope_tables(s)
    mq, mk, mv = _mla_prep(aux, gq, gkv, wq_main, wq_rot, wkv, cos_t, sin_t)

    oc = _attention(proj, proj, proj, q_col=0, k_col=1, v_col=2, dq=HEAD_DIM,
                    slopes=_alibi_slopes(N_HEADS), diff=True, lam=lam_params, subg=subln_g,
                    lam_init=lam_init, name="diff_attn")
    od = _attention(mq, mk, mv, q_col=0, k_col=0, v_col=0, dq=2 * LANE, name="mla_attn")
    wo = w_out.astype(BF16)
    return _proj_ln(oc, od, wo[:HW], wo[HW:], y, g, b)


def kernel(x, ffn_w_gate, ffn_w_up, ffn_w_down, ln_g, ln_b, ev_w_in, ev_w_out, od_w_in, od_w_out,
           od_lambda, od_subln_g, od_q_norm_g, od_kv_norm_g, od_w_uq, od_w_ukv):
    batch = x.shape[0]
    outs = []
    for bi in range(batch):
        y = x[bi]
        for i in range(DEPTH):
            j = i // 2
            y, yb = _ffn_ln(y, ffn_w_gate[i, 0].astype(BF16), ffn_w_up[i, 0].astype(BF16),
                            ffn_w_down[i, 0].astype(BF16), ln_g[i, 0], ln_b[i, 0])
            if i % 2 == 0:
                y, yb = _even_mixer(y, yb, ev_w_in[j], ev_w_out[j], ln_g[i, 1], ln_b[i, 1])
            else:
                lam_init = 0.8 - 0.6 * math.exp(-0.3 * i)
                y, yb = _odd_mixer(y, yb, od_w_in[j], od_w_out[j], od_lambda[j], od_subln_g[j],
                                   od_q_norm_g[j], od_kv_norm_g[j], od_w_uq[j], od_w_ukv[j], lam_init,
                                   ln_g[i, 1], ln_b[i, 1])
            y, yb = _ffn_ln(y, ffn_w_gate[i, 1].astype(BF16), ffn_w_up[i, 1].astype(BF16),
                            ffn_w_down[i, 1].astype(BF16), ln_g[i, 2], ln_b[i, 2])
        outs.append(y)
    return jnp.stack(outs)
```

```python
import functools
import math

import numpy as np
import jax
import jax.numpy as jnp
from jax import lax
from jax.experimental import pallas as pl
from jax.experimental.pallas import tpu as pltpu

F32 = jnp.float32
BF16 = jnp.bfloat16

D_MODEL = 2048
DEPTH = 2
HEAD_DIM = 128
N_HEADS = 8
A_PATTERNS = ((128, 1), (512, 4), (2048, 16))
IDX_HEADS = 16
IDX_DIM = 64
B_TOPK_MAX = 256
C_QK_DIM = 64
Q_LORA = 768
KV_LORA = 512
NOPE_DIM = 128
ROPE_DIM = 64
V_DIM = 128
ROPE_THETA = 10000.0
D_FF = 5632
LN_EPS = 1e-5
RMS_EPS = 1e-6
DIFF_EPS = 1e-5
ALPHA = (2 * DEPTH) ** 0.25

LANE = 128
NEG = -1e30
LOG2E = math.log2(math.e)
VMEM_LIMIT = 56 * 1024 * 1024

HW = N_HEADS * HEAD_DIM
ATT_TQ = 1024
ATT_TK = 512


def _cparams(sem):
    return pltpu.CompilerParams(dimension_semantics=sem, vmem_limit_bytes=VMEM_LIMIT)


def _layernorm_rows(z, g, b):
    mu = jnp.mean(z, axis=-1, keepdims=True)
    zc = z - mu
    var = jnp.mean(zc * zc, axis=-1, keepdims=True)
    return zc * lax.rsqrt(var + LN_EPS) * g + b


def _ffn_ln_kernel(x_ref, wg_ref, wu_ref, wd_ref, g_ref, b_ref, y_ref, yb_ref, xb_sc, acc_sc):
    j = pl.program_id(1)

    @pl.when(j == 0)
    def _():
        xb_sc[...] = x_ref[...].astype(BF16)
        acc_sc[...] = jnp.zeros_like(acc_sc)

    xb = xb_sc[...]
    gate = jnp.dot(xb, wg_ref[...], preferred_element_type=F32)
    up = jnp.dot(xb, wu_ref[...], preferred_element_type=F32)
    h = (gate * jax.nn.sigmoid(gate)) * up
    acc_sc[...] += jnp.dot(h.astype(BF16), wd_ref[...], preferred_element_type=F32)

    @pl.when(j == pl.num_programs(1) - 1)
    def _():
        z = ALPHA * x_ref[...] + 0.5 * acc_sc[...]
        y = _layernorm_rows(z, g_ref[...], b_ref[...])
        y_ref[...] = y
        yb_ref[...] = y.astype(BF16)


def _ffn_ln(x, wg, wu, wd, g, b, *, tm=512, tf=512):
    s, d = x.shape
    dff = wg.shape[1]
    return pl.pallas_call(
        _ffn_ln_kernel,
        grid=(s // tm, dff // tf),
        in_specs=[
            pl.BlockSpec((tm, d), lambda i, j: (i, 0)),
            pl.BlockSpec((d, tf), lambda i, j: (0, j)),
            pl.BlockSpec((d, tf), lambda i, j: (0, j)),
            pl.BlockSpec((tf, d), lambda i, j: (j, 0)),
            pl.BlockSpec((1, d), lambda i, j: (0, 0)),
            pl.BlockSpec((1, d), lambda i, j: (0, 0)),
        ],
        out_specs=[
            pl.BlockSpec((tm, d), lambda i, j: (i, 0)),
            pl.BlockSpec((tm, d), lambda i, j: (i, 0)),
        ],
        out_shape=[jax.ShapeDtypeStruct((s, d), F32), jax.ShapeDtypeStruct((s, d), BF16)],
        scratch_shapes=[pltpu.VMEM((tm, d), BF16), pltpu.VMEM((tm, d), F32)],
        compiler_params=_cparams(("parallel", "arbitrary")),
        name="ffn_ln",
    )(x, wg, wu, wd, g.reshape(1, d), b.reshape(1, d))


def _mm_kernel(x_ref, w_ref, s_ref, o_ref):
    acc = jnp.dot(x_ref[...], w_ref[...], preferred_element_type=F32)
    o_ref[...] = (acc * s_ref[...]).astype(o_ref.dtype)


def _matmul(x, w, colscale, out_dtype, *, tm=1024, tn=512, name="proj"):
    m, k = x.shape
    n = w.shape[1]
    tm, tn = min(tm, m), min(tn, n)
    assert m % tm == 0 and n % tn == 0
    return pl.pallas_call(
        _mm_kernel,
        grid=(m // tm, n // tn),
        in_specs=[
            pl.BlockSpec((tm, k), lambda i, j: (i, 0)),
            pl.BlockSpec((k, tn), lambda i, j: (0, j)),
            pl.BlockSpec((1, tn), lambda i, j: (0, j)),
        ],
        out_specs=pl.BlockSpec((tm, tn), lambda i, j: (i, j)),
        out_shape=jax.ShapeDtypeStruct((m, n), out_dtype),
        compiler_params=_cparams(("parallel", "arbitrary")),
        name=name,
    )(x, w, colscale.reshape(1, n))


def _proj_ln_kernel(oa_ref, ob_ref, wa_ref, wb_ref, x_ref, g_ref, b_ref, y_ref, yb_ref):
    mix = jnp.dot(oa_ref[...], wa_ref[...], preferred_element_type=F32)
    mix += jnp.dot(ob_ref[...], wb_ref[...], preferred_element_type=F32)
    y = _layernorm_rows(ALPHA * x_ref[...] + mix, g_ref[...], b_ref[...])
    y_ref[...] = y
    yb_ref[...] = y.astype(BF16)


def _proj_ln(oa, ob, wa, wb, x, g, b, *, tm=512):
    s, d = x.shape
    ka, kb = oa.shape[1], ob.shape[1]
    return pl.pallas_call(
        _proj_ln_kernel,
        grid=(s // tm,),
        in_specs=[
            pl.BlockSpec((tm, ka), lambda i: (i, 0)),
            pl.BlockSpec((tm, kb), lambda i: (i, 0)),
            pl.BlockSpec((ka, d), lambda i: (0, 0)),
            pl.BlockSpec((kb, d), lambda i: (0, 0)),
            pl.BlockSpec((tm, d), lambda i: (i, 0)),
            pl.BlockSpec((1, d), lambda i: (0, 0)),
            pl.BlockSpec((1, d), lambda i: (0, 0)),
        ],
        out_specs=[
            pl.BlockSpec((tm, d), lambda i: (i, 0)),
            pl.BlockSpec((tm, d), lambda i: (i, 0)),
        ],
        out_shape=[jax.ShapeDtypeStruct((s, d), F32), jax.ShapeDtypeStruct((s, d), BF16)],
        compiler_params=_cparams(("parallel",)),
        name="proj_ln",
    )(oa, ob, wa, wb, x, g.reshape(1, d), b.reshape(1, d))


_FIRST, _LAST, _DIAG = 1, 2, 4


def _attn_kernel(qi_ref, ki_ref, bi_ref, fl_ref, *refs, tq, tk, dq, slopes, bias_kind, diff,
                 lam_init):
    refs = list(refs)
    q_ref, k_ref, v_ref = refs[:3]
    pos = 3
    bias_ref = None
    if bias_kind is not None:
        bias_ref = refs[pos]
        pos += 1
    if diff:
        lam_ref, subg_ref = refs[pos], refs[pos + 1]
        pos += 2
    o_ref = refs[pos]
    m_sc, l_sc, acc_sc = refs[pos + 1:pos + 4]
    q2_sc = refs[pos + 4] if diff else None

    p = pl.program_id(0)
    fl = fl_ref[p]
    q0 = qi_ref[p] * tq
    k0 = ki_ref[p] * tk
    rows = 2 * tq if diff else tq

    @pl.when((fl & _FIRST) != 0)
    def _():
        m_sc[...] = jnp.full_like(m_sc, NEG)
        l_sc[...] = jnp.zeros_like(l_sc)
        acc_sc[...] = jnp.zeros_like(acc_sc)
        if diff:
            lane = lax.broadcasted_iota(jnp.int32, (tq, HEAD_DIM), 1)
            for h in range(N_HEADS):
                qh = q_ref[:, h * HEAD_DIM:(h + 1) * HEAD_DIM]
                zero = jnp.zeros_like(qh)
                q2_sc[h, :tq, :] = jnp.where(lane < C_QK_DIM, qh, zero)
                q2_sc[h, tq:, :] = jnp.where(lane >= C_QK_DIM, qh, zero)

    def step(masked):
        if slopes is not None:
            kp = (k0 - q0).astype(F32) + lax.broadcasted_iota(jnp.int32, (1, tk), 1).astype(F32)
        if bias_ref is not None:
            bias = bias_ref[...].reshape(tq, tk).astype(F32)
        if masked:
            r = lax.broadcasted_iota(jnp.int32, (rows, tk), 0)
            if diff:
                r = jnp.where(r >= tq, r - tq, r)
            c = lax.broadcasted_iota(jnp.int32, (rows, tk), 1)
            keep = (r - c) >= (k0 - q0)
        ones = jnp.ones((tk, LANE), BF16)
        for h in range(N_HEADS):
            if diff:
                q = q2_sc[h]
            else:
                q = q_ref[:, h * dq:(h + 1) * dq]
            k = k_ref[:, h * dq:(h + 1) * dq]
            s = lax.dot_general(q, k, (((1,), (1,)), ((), ())), preferred_element_type=F32)
            if slopes is not None:
                s = s + (slopes[h] * LOG2E) * kp
            if bias_ref is not None:
                s = s + bias
            if masked:
                s = jnp.where(keep, s, NEG)
            m_prev = m_sc[h]
            m_new = jnp.maximum(m_prev, jnp.max(s, axis=-1, keepdims=True))
            alpha = jnp.exp2(m_prev - m_new)
            pexp = jnp.exp2(s - jnp.concatenate([m_new] * (tk // LANE), axis=-1))
            v_aug = jnp.concatenate([v_ref[:, h * HEAD_DIM:(h + 1) * HEAD_DIM], ones], axis=-1)
            pv = jnp.dot(pexp.astype(BF16), v_aug, preferred_element_type=F32)
            l_sc[h] = alpha * l_sc[h] + pv[:, HEAD_DIM:]
            acc_sc[h] = alpha * acc_sc[h] + pv[:, :HEAD_DIM]
            m_sc[h] = m_new

    if bias_kind is None:
        @pl.when((fl & _DIAG) != 0)
        def _():
            step(True)

        @pl.when((fl & _DIAG) == 0)
        def _():
            step(False)
    else:
        step(False)

    @pl.when((fl & _LAST) != 0)
    def _():
        if diff:
            lp = lam_ref[...]
            lam = (jnp.exp(jnp.sum(lp[0:1] * lp[1:2], axis=-1, keepdims=True))
                   - jnp.exp(jnp.sum(lp[2:3] * lp[3:4], axis=-1, keepdims=True)) + lam_init)
            for h in range(N_HEADS):
                o1 = acc_sc[h, :tq, :] / l_sc[h, :tq, :]
                o2 = acc_sc[h, tq:, :] / l_sc[h, tq:, :]
                o = o1 - lam * o2
                ms = jnp.mean(o * o, axis=-1, keepdims=True)
                o = o * lax.rsqrt(ms + DIFF_EPS) * subg_ref[...]
                o_ref[:, h * HEAD_DIM:(h + 1) * HEAD_DIM] = (o * (1.0 - lam_init)).astype(o_ref.dtype)
        else:
            for h in range(N_HEADS):
                o_ref[:, h * HEAD_DIM:(h + 1) * HEAD_DIM] = (acc_sc[h] / l_sc[h]).astype(o_ref.dtype)


def _causal_pairs(nq, ratio=1, lookback=None):
    qi, ki, bi, fl = [], [], [], []
    for q in range(nq):
        lo = 0 if lookback is None else max(0, q * ratio - lookback)
        hi = (q + 1) * ratio - 1
        for k in range(lo, hi + 1):
            qi.append(q)
            ki.append(k)
            bi.append(q * ratio - k)
            fl.append((_FIRST if k == lo else 0) | (_LAST if k == hi else 0)
                      | (_DIAG if k >= q * ratio else 0))
    return tuple(np.asarray(a, np.int32) for a in (qi, ki, bi, fl))


def _attention(q_arr, k_arr, v_arr, *, q_col, k_col, v_col, dq, slopes=None, bias=None, bias_kind=None,
               lookback=None, diff=False, lam=None, subg=None, lam_init=0.0, tq=512, tk=512, name="attn"):
    s = q_arr.shape[0]
    tq = min(tq, s)
    assert tq % tk == 0 and s % tq == 0 and (bias_kind != "table" or tq == tk)
    nq = s // tq
    qi, ki, bi, fl = _causal_pairs(nq, tq // tk, lookback)
    qw = N_HEADS * (HEAD_DIM if diff else dq)
    kw = N_HEADS * dq
    in_specs = [
        pl.BlockSpec((tq, qw), lambda p, qi, ki, bi, fl: (qi[p], q_col)),
        pl.BlockSpec((tk, kw), lambda p, qi, ki, bi, fl: (ki[p], k_col)),
        pl.BlockSpec((tk, HW), lambda p, qi, ki, bi, fl: (ki[p], v_col)),
    ]
    args = [q_arr, k_arr, v_arr]
    if bias_kind == "table":
        in_specs.append(pl.BlockSpec((1, tq, tk), lambda p, qi, ki, bi, fl: (bi[p], 0, 0)))
        args.append(bias)
    elif bias_kind == "mask":
        in_specs.append(pl.BlockSpec((tq, tk), lambda p, qi, ki, bi, fl: (qi[p], ki[p])))
        args.append(bias)
    rows = 2 * tq if diff else tq
    scratch = [pltpu.VMEM((N_HEADS, rows, LANE), F32), pltpu.VMEM((N_HEADS, rows, LANE), F32),
               pltpu.VMEM((N_HEADS, rows, HEAD_DIM), F32)]
    if diff:
        in_specs.append(pl.BlockSpec((4, C_QK_DIM), lambda p, qi, ki, bi, fl: (0, 0)))
        in_specs.append(pl.BlockSpec((1, HEAD_DIM), lambda p, qi, ki, bi, fl: (0, 0)))
        args += [lam, subg.reshape(1, HEAD_DIM)]
        scratch.append(pltpu.VMEM((N_HEADS, rows, HEAD_DIM), BF16))
    kern = functools.partial(_attn_kernel, tq=tq, tk=tk, dq=dq, slopes=slopes, bias_kind=bias_kind,
                             diff=diff, lam_init=lam_init)
    return pl.pallas_call(
        kern,
        grid_spec=pltpu.PrefetchScalarGridSpec(
            num_scalar_prefetch=4,
            grid=(len(qi),),
            in_specs=in_specs,
            out_specs=pl.BlockSpec((tq, HW), lambda p, qi, ki, bi, fl: (qi[p], 0)),
            scratch_shapes=scratch,
        ),
        out_shape=jax.ShapeDtypeStruct((s, HW), BF16),
        compiler_params=_cparams(("arbitrary",)),
        name=name,
    )(jnp.asarray(qi), jnp.asarray(ki), jnp.asarray(bi), jnp.asarray(fl), *args)


def _alibi_slopes(n):
    return tuple(2.0 ** (-8.0 * (h + 1) / n) for h in range(n))


def _dilated_bias_table(t):
    max_win = max(w for w, _ in A_PATTERNS)
    n_off = max_win // t + 1
    r = np.arange(t)[:, None]
    c = np.arange(t)[None, :]
    out = np.empty((n_off, t, t), np.float32)
    for o in range(n_off):
        dist = o * t + r - c
        mult = np.zeros((t, t), np.float64)
        for win, dil in A_PATTERNS:
            mult += (dist >= 0) & (dist <= win) & (dist % dil == 0)
        out[o] = np.where(mult > 0, np.log2(np.maximum(mult, 1.0)), NEG)
    return out


def _sortable_key(x):
    bits = pltpu.bitcast(x, jnp.int32)
    return jnp.where(bits < 0, bits ^ jnp.int32(0x7FFFFFFF), bits)


_CAND = 16


def _bitonic_sort_desc(x):
    x = list(x)
    n = len(x)
    k = 2
    while k <= n:
        j = k // 2
        while j >= 1:
            for i in range(n):
                l = i ^ j
                if l > i:
                    hi, lo = jnp.maximum(x[i], x[l]), jnp.minimum(x[i], x[l])
                    x[i], x[l] = (hi, lo) if (i & k) == 0 else (lo, hi)
            j //= 2
        k *= 2
    return x


def _merge_top_desc(a, b):
    n = len(a)
    c = [jnp.maximum(a[i], b[n - 1 - i]) for i in range(n)]
    j = n // 2
    while j >= 1:
        for i in range(n):
            l = i ^ j
            if l > i:
                c[i], c[l] = jnp.maximum(c[i], c[l]), jnp.minimum(c[i], c[l])
        j //= 2
    return c


def _indexer_kernel(q_ref, e1_ref, e2_ref, w_ref, o_ref, key_sc, cand_sc, thr_sc, nge_sc, *, tq, kc, topk,
                    wscale):
    qb = pl.program_id(0)
    q0 = qb * tq
    s_total = o_ref.shape[1]
    n_chunks = (q0 + tq + kc - 1) // kc
    w = w_ref[...] * wscale
    row = lax.broadcasted_iota(jnp.int32, (tq, kc), 0)
    col = lax.broadcasted_iota(jnp.int32, (tq, kc), 1)
    rel = row - col

    def score_chunk(c, carry):
        k0 = pl.multiple_of(c * kc, kc)
        k1 = e1_ref[pl.ds(k0, kc), :]
        k2 = e2_ref[pl.ds(k0, kc), :]
        acc = jnp.zeros((tq, kc), F32)
        for j in range(IDX_HEADS // 2):
            qp = q_ref[:, j * LANE:(j + 1) * LANE]
            s_even = lax.dot_general(qp, k1, (((1,), (1,)), ((), ())), preferred_element_type=F32)
            s_odd = lax.dot_general(qp, k2, (((1,), (1,)), ((), ())), preferred_element_type=F32)
            acc = acc + jnp.maximum(s_even, 0.0) * w[:, 2 * j:2 * j + 1]
            acc = acc + jnp.maximum(s_odd, 0.0) * w[:, 2 * j + 1:2 * j + 2]
        acc = jnp.where(acc == 0.0, 0.0, acc)
        key_sc[:, pl.ds(k0, kc)] = jnp.where(rel >= k0 - q0, acc, -jnp.inf)
        return carry

    lax.fori_loop(0, n_chunks, score_chunk, 0)

    int_min = jnp.int32(-2 ** 31)
    n_tiles = n_chunks * (kc // LANE)
    n_groups = (n_tiles + _CAND - 1) // _CAND

    def pad_tile(t, carry):
        key_sc[:, pl.ds(pl.multiple_of(t * LANE, LANE), LANE)] = jnp.full((tq, LANE), -jnp.inf, F32)
        return carry

    lax.fori_loop(n_tiles, n_groups * _CAND, pad_tile, 0)

    def row_group(r, carry):
        r0 = pl.multiple_of(r * 8, 8)

        def tile_group(g, top):
            base = g * (_CAND * LANE)
            x = [key_sc[pl.ds(r0, 8), pl.ds(pl.multiple_of(base + j * LANE, LANE), LANE)]
                 for j in range(_CAND)]
            return tuple(_merge_top_desc(list(top), _bitonic_sort_desc(x)))

        top0 = tuple(jnp.full((8, LANE), -jnp.inf, F32) for _ in range(_CAND))
        top = lax.fori_loop(0, n_groups, tile_group, top0)
        for j in range(_CAND):
            cand_sc[pl.ds(r0, 8), j * LANE:(j + 1) * LANE] = _sortable_key(top[j])
        return carry

    lax.fori_loop(0, tq // 8, row_group, 0)

    def count_cand(cand):
        cand_b = jnp.broadcast_to(cand, (tq, LANE))
        cnt = jnp.zeros((tq, LANE), jnp.int32)
        for j in range(_CAND):
            cnt = cnt + jnp.where(cand_sc[:, j * LANE:(j + 1) * LANE] >= cand_b, 1, 0)
        return jnp.sum(cnt, axis=-1, keepdims=True)

    def count_all(cand):
        cand_b = jnp.broadcast_to(cand, (tq, LANE))

        def body(t, cnt):
            blk = _sortable_key(key_sc[:, pl.ds(pl.multiple_of(t * LANE, LANE), LANE)])
            return cnt + jnp.where(blk >= cand_b, 1, 0)

        cnt = lax.fori_loop(0, n_tiles, body, jnp.zeros((tq, LANE), jnp.int32))
        return jnp.sum(cnt, axis=-1, keepdims=True)

    def kth_largest(count_ge, lo, hi):
        n_bits = jnp.max(32 - lax.clz(lo ^ hi))
        keep = jnp.where(n_bits >= 32, 0, -lax.shift_left(jnp.int32(1), jnp.minimum(n_bits, 31)))
        floor = jnp.where(n_bits >= 32, int_min, lo & keep)

        def bit_step(i, thr):
            cand = thr + lax.shift_left(jnp.int32(1), n_bits - 1 - i)
            return jnp.where(count_ge(cand) >= topk, cand, thr)

        return lax.fori_loop(0, n_bits, bit_step, floor)

    lo_tile = (topk + LANE - 1) // LANE - 1
    hi_tile = (topk - 1) // LANE
    lo = jnp.min(cand_sc[:, lo_tile * LANE:(lo_tile + 1) * LANE], axis=-1, keepdims=True)
    hi = jnp.max(cand_sc[:, hi_tile * LANE:(hi_tile + 1) * LANE], axis=-1, keepdims=True)
    thr = kth_largest(count_cand, lo, hi)
    thr_sc[...] = jnp.broadcast_to(thr, (tq, LANE))
    nge_sc[...] = jnp.broadcast_to(count_cand(thr), (tq, LANE))
    smallest_kept = cand_sc[:, (_CAND - 1) * LANE:]
    unsafe = jnp.max(jnp.where(smallest_kept >= thr_sc[...], 1, 0)) > 0

    @pl.when(jnp.logical_and(n_groups > 1, unsafe))
    def _():
        everything = (jnp.full((tq, 1), int_min, jnp.int32), jnp.full((tq, 1), 2 ** 31 - 1, jnp.int32))
        thr_all = kth_largest(count_all, *everything)
        thr_sc[...] = jnp.broadcast_to(thr_all, (tq, LANE))
        nge_sc[...] = jnp.broadcast_to(count_all(thr_all), (tq, LANE))

    thr_key = thr_sc[...]
    thr_b = pltpu.bitcast(jnp.where(thr_key < 0, thr_key ^ jnp.int32(0x7FFFFFFF), thr_key), F32)
    row_t = lax.broadcasted_iota(jnp.int32, (tq, LANE), 0)
    col_t = lax.broadcasted_iota(jnp.int32, (tq, LANE), 1)
    rel_t = row_t - col_t
    tied = jnp.max(jnp.where(nge_sc[...] > topk, 1, 0)) > 0

    @pl.when(jnp.logical_not(tied))
    def _():
        def mask_tile(t, carry):
            k0 = pl.multiple_of(t * LANE, LANE)
            sel = (key_sc[:, pl.ds(k0, LANE)] >= thr_b) & (rel_t >= k0 - q0)
            o_ref[:, pl.ds(k0, LANE)] = jnp.where(sel, 0.0, NEG).astype(o_ref.dtype)
            return carry

        lax.fori_loop(0, n_tiles, mask_tile, 0)

    @pl.when(tied)
    def _():
        need = topk - count_all(thr_key[:, :1] + 1)

        def count_tied_upto(pos_max):
            pos_b = jnp.broadcast_to(pos_max, (tq, LANE))

            def body(t, cnt):
                k0 = pl.multiple_of(t * LANE, LANE)
                blk = _sortable_key(key_sc[:, pl.ds(k0, LANE)])
                return cnt + jnp.where((blk == thr_key) & (col_t + k0 <= pos_b), 1, 0)

            cnt = lax.fori_loop(0, n_tiles, body, jnp.zeros((tq, LANE), jnp.int32))
            return jnp.sum(cnt, axis=-1, keepdims=True)

        n_pos_bits = max(1, (key_sc.shape[1] - 1).bit_length())

        def bit_step(i, cut):
            cand = cut + lax.shift_left(jnp.int32(1), n_pos_bits - 1 - i)
            return jnp.where(count_tied_upto(cand - 1) < need, cand, cut)

        cut = lax.fori_loop(0, n_pos_bits, bit_step, jnp.zeros((tq, 1), jnp.int32))
        cut_b = jnp.broadcast_to(cut, (tq, LANE))

        def mask_tile(t, carry):
            k0 = pl.multiple_of(t * LANE, LANE)
            blk = _sortable_key(key_sc[:, pl.ds(k0, LANE)])
            sel = (blk > thr_key) | ((blk == thr_key) & (col_t + k0 <= cut_b))
            sel = sel & (rel_t >= k0 - q0)
            o_ref[:, pl.ds(k0, LANE)] = jnp.where(sel, 0.0, NEG).astype(o_ref.dtype)
            return carry

        lax.fori_loop(0, n_tiles, mask_tile, 0)

    def fill_chunk(c, carry):
        k0 = pl.multiple_of(c * kc, kc)
        o_ref[:, pl.ds(k0, kc)] = jnp.full((tq, kc), NEG, o_ref.dtype)
        return carry

    lax.fori_loop(n_chunks, s_total // kc, fill_chunk, 0)


def _indexer_mask(proj, iw, *, q_col, e1_col, e2_col, topk, tq=256, kc=512):
    s = proj.shape[0]
    assert topk <= _CAND * LANE
    kern = functools.partial(_indexer_kernel, tq=tq, kc=kc, topk=topk,
                             wscale=(IDX_HEADS * IDX_DIM) ** -0.5)
    return pl.pallas_call(
        kern,
        grid=(s // tq,),
        in_specs=[
            pl.BlockSpec((tq, IDX_HEADS * IDX_DIM), lambda i: (i, q_col)),
            pl.BlockSpec((s, LANE), lambda i: (0, e1_col)),
            pl.BlockSpec((s, LANE), lambda i: (0, e2_col)),
            pl.BlockSpec((tq, LANE), lambda i: (i, 0)),
        ],
        out_specs=pl.BlockSpec((tq, s), lambda i: (i, 0)),
        out_shape=jax.ShapeDtypeStruct((s, s), BF16),
        scratch_shapes=[pltpu.VMEM((tq, -(-s // (_CAND * LANE)) * _CAND * LANE), F32),
                        pltpu.VMEM((tq, _CAND * LANE), jnp.int32),
                        pltpu.VMEM((tq, LANE), jnp.int32),
                        pltpu.VMEM((tq, LANE), jnp.int32)],
        compiler_params=_cparams(("parallel",)),
        name="indexer_topk",
    )(proj, proj, proj, iw)


def _rms_rows(x, g, eps):
    return x * lax.rsqrt(jnp.mean(x * x, axis=-1, keepdims=True) + eps) * g


def _mla_q_kernel(cq_ref, g_ref, wm_ref, wr_ref, cos_ref, sin_ref, o_ref, *, scale):
    n = _rms_rows(cq_ref[...], g_ref[...], RMS_EPS).astype(BF16)
    main = jnp.dot(n, wm_ref[...], preferred_element_type=F32)
    rot = jnp.dot(n, wr_ref[...], preferred_element_type=F32)
    cos, sin = cos_ref[...], sin_ref[...]
    for h in range(N_HEADS):
        base = h * 2 * LANE
        o_ref[:, base:base + LANE] = (main[:, base:base + LANE] * scale).astype(o_ref.dtype)
        pe = main[:, base + LANE:base + 2 * LANE] * cos + rot[:, h * LANE:(h + 1) * LANE] * sin
        o_ref[:, base + LANE:base + 2 * LANE] = (pe * scale).astype(o_ref.dtype)


def _mla_kv_kernel(ckv_ref, ef_ref, g_ref, w_ref, cos_ref, sin_ref, k_ref, v_ref):
    n = _rms_rows(ckv_ref[...], g_ref[...], RMS_EPS).astype(BF16)
    kv = jnp.dot(n, w_ref[...], preferred_element_type=F32)
    ef = ef_ref[...]
    kpe = (ef[:, :LANE] * cos_ref[...] + ef[:, LANE:] * sin_ref[...]).astype(k_ref.dtype)
    for h in range(N_HEADS):
        base = h * 2 * LANE
        k_ref[:, base:base + LANE] = kv[:, h * LANE:(h + 1) * LANE].astype(k_ref.dtype)
        k_ref[:, base + LANE:base + 2 * LANE] = kpe
    v_ref[...] = kv[:, HW:].astype(v_ref.dtype)


def _mla_prep(aux, gq, gkv, wq_main, wq_rot, wkv, cos_t, sin_t, *, tm=512):
    s = aux.shape[0]
    scale = (NOPE_DIM + ROPE_DIM) ** -0.5 * LOG2E
    q = pl.pallas_call(
        functools.partial(_mla_q_kernel, scale=scale),
        grid=(s // tm,),
        in_specs=[
            pl.BlockSpec((tm, Q_LORA), lambda i: (i, 1)),
            pl.BlockSpec((1, Q_LORA), lambda i: (0, 0)),
            pl.BlockSpec(wq_main.shape, lambda i: (0, 0)),
            pl.BlockSpec(wq_rot.shape, lambda i: (0, 0)),
            pl.BlockSpec((tm, LANE), lambda i: (i, 0)),
            pl.BlockSpec((tm, LANE), lambda i: (i, 0)),
        ],
        out_specs=pl.BlockSpec((tm, 2 * HW), lambda i: (i, 0)),
        out_shape=jax.ShapeDtypeStruct((s, 2 * HW), BF16),
        compiler_params=_cparams(("parallel",)),
        name="mla_q_prep",
    )(aux, gq.reshape(1, Q_LORA), wq_main, wq_rot, cos_t, sin_t)
    k, v = pl.pallas_call(
        _mla_kv_kernel,
        grid=(s // tm,),
        in_specs=[
            pl.BlockSpec((tm, KV_LORA), lambda i: (i, 0)),
            pl.BlockSpec((tm, 2 * LANE), lambda i: (i, 2)),
            pl.BlockSpec((1, KV_LORA), lambda i: (0, 0)),
            pl.BlockSpec(wkv.shape, lambda i: (0, 0)),
            pl.BlockSpec((tm, LANE), lambda i: (i, 0)),
            pl.BlockSpec((tm, LANE), lambda i: (i, 0)),
        ],
        out_specs=[pl.BlockSpec((tm, 2 * HW), lambda i: (i, 0)), pl.BlockSpec((tm, HW), lambda i: (i, 0))],
        out_shape=[jax.ShapeDtypeStruct((s, 2 * HW), BF16), jax.ShapeDtypeStruct((s, HW), BF16)],
        compiler_params=_cparams(("parallel",)),
        name="mla_kv_prep",
    )(aux, aux, gkv.reshape(1, KV_LORA), wkv, cos_t, sin_t)
    return q, k, v


def _rope_tables(s):
    half = ROPE_DIM // 2
    inv = ROPE_THETA ** (-jnp.arange(half, dtype=F32) / half)
    ang = jnp.arange(s, dtype=jnp.int32).astype(F32)[:, None] * inv[None, :]
    cos, sin = jnp.cos(ang), jnp.sin(ang)
    pad = jnp.zeros((s, LANE - ROPE_DIM), F32)
    return (jnp.concatenate([cos, cos, pad], -1), jnp.concatenate([-sin, sin, pad], -1))


def _swap_halves(w):
    half = w.shape[-1] // 2
    return jnp.concatenate([w[..., half:], w[..., :half]], -1)


def _even_mixer(y, yb, w_in, w_out, g, b):
    s = y.shape[0]
    d = w_in.shape[0]
    att_scale = HEAD_DIM ** -0.5 * LOG2E
    n_qkv = 6 * HW
    iq_w = w_in[:, n_qkv:n_qkv + IDX_HEADS * IDX_DIM]
    ik_w = w_in[:, n_qkv + IDX_HEADS * IDX_DIM:n_qkv + IDX_HEADS * IDX_DIM + IDX_DIM]
    iw_w = w_in[:, n_qkv + IDX_HEADS * IDX_DIM + IDX_DIM:]
    z64 = jnp.zeros((d, LANE - IDX_DIM), F32)
    w_main = jnp.concatenate([w_in[:, :n_qkv], iq_w, ik_w, z64, z64, ik_w, jnp.zeros((d, 2 * LANE), F32)],
                             -1).astype(BF16)
    ones = jnp.ones((HW,), F32)
    colscale = jnp.concatenate([ones * att_scale, ones, ones, ones * att_scale, ones, ones, ones,
                                jnp.ones((4 * LANE,), F32)])
    proj = _matmul(yb, w_main, colscale, BF16, name="even_in_proj")
    w_iw = jnp.concatenate([iw_w, jnp.zeros((d, LANE - IDX_HEADS), F32)], -1).astype(BF16)
    iw = _matmul(yb, w_iw, jnp.ones((LANE,), F32), F32, tn=LANE, name="even_iw_proj")

    slopes = _alibi_slopes(N_HEADS)
    t = 512
    table = jnp.asarray(_dilated_bias_table(t))
    lookback = max(w for w, _ in A_PATTERNS) // t
    oa = _attention(proj, proj, proj, q_col=0, k_col=1, v_col=2, dq=HEAD_DIM, slopes=slopes,
                    bias=table, bias_kind="table", lookback=lookback, tq=t, tk=t, name="dilated_attn")

    topk = min(B_TOPK_MAX, s // 4)
    mask = _indexer_mask(proj, iw, q_col=6, e1_col=7 * HW // LANE, e2_col=7 * HW // LANE + 1, topk=topk)
    ob = _attention(proj, proj, proj, q_col=3, k_col=4, v_col=5, dq=HEAD_DIM, slopes=slopes,
                    bias=mask, bias_kind="mask", tq=ATT_TQ, tk=ATT_TK, name="dsa_attn")
    wo = w_out.astype(BF16)
    return _proj_ln(oa, ob, wo[:HW], wo[HW:], y, g, b)


def _odd_mixer(y, yb, w_in, w_out, lam_params, subln_g, gq, gkv, w_uq, w_ukv, lam_init, g, b):
    s = y.shape[0]
    d = w_in.shape[0]
    w_c = w_in[:, :3 * HW].astype(BF16)
    c_scale = jnp.concatenate([jnp.full((HW,), C_QK_DIM ** -0.5 * LOG2E, F32), jnp.ones((2 * HW,), F32)])
    proj = _matmul(yb, w_c, c_scale, BF16, name="odd_in_proj")
    mq_w = w_in[:, 3 * HW:3 * HW + Q_LORA]
    mkv_w = w_in[:, 3 * HW + Q_LORA:3 * HW + Q_LORA + KV_LORA]
    kpe_w = w_in[:, 3 * HW + Q_LORA + KV_LORA:]
    z64 = jnp.zeros((d, LANE - ROPE_DIM), F32)
    w_aux = jnp.concatenate([mkv_w, kpe_w, z64, _swap_halves(kpe_w), z64, mq_w], -1).astype(BF16)
    aux = _matmul(yb, w_aux, jnp.ones((w_aux.shape[1],), F32), F32, name="odd_aux_proj")

    wq = w_uq.reshape(Q_LORA, N_HEADS, NOPE_DIM + ROPE_DIM)
    zq = jnp.zeros((Q_LORA, N_HEADS, LANE - ROPE_DIM), F32)
    wq_main = jnp.concatenate([wq, zq], -1).reshape(Q_LORA, 2 * HW).astype(BF16)
    wq_rot = jnp.concatenate([_swap_halves(wq[..., NOPE_DIM:]), zq], -1).reshape(Q_LORA, HW).astype(BF16)
    wkv = w_ukv.reshape(KV_LORA, N_HEADS, NOPE_DIM + V_DIM)
    wkv = jnp.concatenate([wkv[..., :NOPE_DIM].reshape(KV_LORA, HW),
                           wkv[..., NOPE_DIM:].reshape(KV_LORA, HW)], -1).astype(BF16)
    cos_t, sin_t = _rope_tables(s)
    mq, mk, mv = _mla_prep(aux, gq, gkv, wq_main, wq_rot, wkv, cos_t, sin_t)

    oc = _attention(proj, proj, proj, q_col=0, k_col=1, v_col=2, dq=HEAD_DIM,
                    slopes=_alibi_slopes(N_HEADS), diff=True, lam=lam_params, subg=subln_g,
                    lam_init=lam_init, tq=ATT_TQ, tk=ATT_TK, name="diff_attn")
    od = _attention(mq, mk, mv, q_col=0, k_col=0, v_col=0, dq=2 * LANE, tq=ATT_TQ, tk=ATT_TK,
                    name="mla_attn")
    wo = w_out.astype(BF16)
    return _proj_ln(oc, od, wo[:HW], wo[HW:], y, g, b)


def kernel(x, ffn_w_gate, ffn_w_up, ffn_w_down, ln_g, ln_b, ev_w_in, ev_w_out, od_w_in, od_w_out,
           od_lambda, od_subln_g, od_q_norm_g, od_kv_norm_g, od_w_uq, od_w_ukv):
    batch, seq, d_model = x.shape
    xs = x.reshape(batch * seq, d_model)
    outs = []
    for bi in range(batch):
        y = xs if batch == 1 else xs[bi * seq:(bi + 1) * seq]
        for i in range(DEPTH):
            j = i // 2
            y, yb = _ffn_ln(y, ffn_w_gate[i, 0].astype(BF16), ffn_w_up[i, 0].astype(BF16),
                            ffn_w_down[i, 0].astype(BF16), ln_g[i, 0], ln_b[i, 0])
            if i % 2 == 0:
                y, yb = _even_mixer(y, yb, ev_w_in[j], ev_w_out[j], ln_g[i, 1], ln_b[i, 1])
            else:
                lam_init = 0.8 - 0.6 * math.exp(-0.3 * i)
                y, yb = _odd_mixer(y, yb, od_w_in[j], od_w_out[j], od_lambda[j], od_subln_g[j],
                                   od_q_norm_g[j], od_kv_norm_g[j], od_w_uq[j], od_w_ukv[j], lam_init,
                                   ln_g[i, 1], ln_b[i, 1])
            y, yb = _ffn_ln(y, ffn_w_gate[i, 1].astype(BF16), ffn_w_up[i, 1].astype(BF16),
                            ffn_w_down[i, 1].astype(BF16), ln_g[i, 2], ln_b[i, 2])
        outs.append(y)
    out = outs[0] if batch == 1 else jnp.concatenate(outs, 0)
    return out.reshape(batch, seq, d_model)
```

```python
import functools
import math

import numpy as np
import jax
import jax.numpy as jnp
from jax import lax
from jax.experimental import pallas as pl
from jax.experimental.pallas import tpu as pltpu

F32 = jnp.float32
BF16 = jnp.bfloat16

D_MODEL = 2048
DEPTH = 2
HEAD_DIM = 128
N_HEADS = 8
A_PATTERNS = ((128, 1), (512, 4), (2048, 16))
IDX_HEADS = 16
IDX_DIM = 64
B_TOPK_MAX = 256
C_QK_DIM = 64
Q_LORA = 768
KV_LORA = 512
NOPE_DIM = 128
ROPE_DIM = 64
V_DIM = 128
ROPE_THETA = 10000.0
D_FF = 5632
LN_EPS = 1e-5
RMS_EPS = 1e-6
DIFF_EPS = 1e-5
ALPHA = (2 * DEPTH) ** 0.25

LANE = 128
NEG = -1e30
LOG2E = math.log2(math.e)
VMEM_LIMIT = 56 * 1024 * 1024

HW = N_HEADS * HEAD_DIM
ATT_TQ = 1024
ATT_TK = 512
ATT_NSUB = 2


def _cparams(sem):
    return pltpu.CompilerParams(dimension_semantics=sem, vmem_limit_bytes=VMEM_LIMIT)


def _layernorm_rows(z, g, b):
    mu = jnp.mean(z, axis=-1, keepdims=True)
    zc = z - mu
    var = jnp.mean(zc * zc, axis=-1, keepdims=True)
    return zc * lax.rsqrt(var + LN_EPS) * g + b


def _ffn_ln_kernel(xb_ref, x_ref, wg_ref, wu_ref, wd_ref, g_ref, b_ref, y_ref, yb_ref, acc_sc):
    j = pl.program_id(1)

    @pl.when(j == 0)
    def _():
        acc_sc[...] = jnp.zeros_like(acc_sc)

    xb = xb_ref[...]
    gate = jnp.dot(xb, wg_ref[...], preferred_element_type=F32)
    up = jnp.dot(xb, wu_ref[...], preferred_element_type=F32)
    h = (gate * jax.nn.sigmoid(gate)) * up
    acc_sc[...] += jnp.dot(h.astype(BF16), wd_ref[...], preferred_element_type=F32)

    @pl.when(j == pl.num_programs(1) - 1)
    def _():
        z = ALPHA * x_ref[...] + 0.5 * acc_sc[...]
        y = _layernorm_rows(z, g_ref[...], b_ref[...])
        y_ref[...] = y
        yb_ref[...] = y.astype(BF16)


def _ffn_ln(x, xb, wg, wu, wd, g, b, *, layer, sub, tm=512, tf=512):
    s, d = x.shape
    dff = wg.shape[-1]
    row = lambda i, j: (i, 0)
    return pl.pallas_call(
        _ffn_ln_kernel,
        grid=(s // tm, dff // tf),
        in_specs=[
            pl.BlockSpec((tm, d), row),
            pl.BlockSpec((tm, d), row),
            pl.BlockSpec((None, None, d, tf), lambda i, j: (layer, sub, 0, j)),
            pl.BlockSpec((None, None, d, tf), lambda i, j: (layer, sub, 0, j)),
            pl.BlockSpec((None, None, tf, d), lambda i, j: (layer, sub, j, 0)),
            pl.BlockSpec((1, d), lambda i, j: (0, 0)),
            pl.BlockSpec((1, d), lambda i, j: (0, 0)),
        ],
        out_specs=[pl.BlockSpec((tm, d), row), pl.BlockSpec((tm, d), row)],
        out_shape=[jax.ShapeDtypeStruct((s, d), F32), jax.ShapeDtypeStruct((s, d), BF16)],
        scratch_shapes=[pltpu.VMEM((tm, d), F32)],
        compiler_params=_cparams(("parallel", "arbitrary")),
        name="ffn_ln",
    )(xb, x, wg, wu, wd, g.reshape(1, d), b.reshape(1, d))


def _mm_kernel(x_ref, w_ref, s_ref, o_ref):
    acc = jnp.dot(x_ref[...], w_ref[...], preferred_element_type=F32)
    o_ref[...] = (acc * s_ref[...]).astype(o_ref.dtype)


def _matmul(x, w, colscale, out_dtype, *, tm=1024, tn=512, name="proj"):
    m, k = x.shape
    n = w.shape[1]
    tm, tn = min(tm, m), min(tn, n)
    assert m % tm == 0 and n % tn == 0
    return pl.pallas_call(
        _mm_kernel,
        grid=(m // tm, n // tn),
        in_specs=[
            pl.BlockSpec((tm, k), lambda i, j: (i, 0)),
            pl.BlockSpec((k, tn), lambda i, j: (0, j)),
            pl.BlockSpec((1, tn), lambda i, j: (0, j)),
        ],
        out_specs=pl.BlockSpec((tm, tn), lambda i, j: (i, j)),
        out_shape=jax.ShapeDtypeStruct((m, n), out_dtype),
        compiler_params=_cparams(("parallel", "arbitrary")),
        name=name,
    )(x, w, colscale.reshape(1, n))


def _proj_ln_kernel(oa_ref, ob_ref, wa_ref, wb_ref, x_ref, g_ref, b_ref, y_ref, yb_ref):
    mix = jnp.dot(oa_ref[...], wa_ref[...], preferred_element_type=F32)
    mix += jnp.dot(ob_ref[...], wb_ref[...], preferred_element_type=F32)
    y = _layernorm_rows(ALPHA * x_ref[...] + mix, g_ref[...], b_ref[...])
    y_ref[...] = y
    yb_ref[...] = y.astype(BF16)


def _proj_ln(oa, ob, w, x, g, b, *, tm=512):
    s, d = x.shape
    ka, kb = oa.shape[1], ob.shape[1]
    assert ka == kb and w.shape[0] == ka + kb
    return pl.pallas_call(
        _proj_ln_kernel,
        grid=(s // tm,),
        in_specs=[
            pl.BlockSpec((tm, ka), lambda i: (i, 0)),
            pl.BlockSpec((tm, kb), lambda i: (i, 0)),
            pl.BlockSpec((ka, d), lambda i: (0, 0)),
            pl.BlockSpec((kb, d), lambda i: (1, 0)),
            pl.BlockSpec((tm, d), lambda i: (i, 0)),
            pl.BlockSpec((1, d), lambda i: (0, 0)),
            pl.BlockSpec((1, d), lambda i: (0, 0)),
        ],
        out_specs=[
            pl.BlockSpec((tm, d), lambda i: (i, 0)),
            pl.BlockSpec((tm, d), lambda i: (i, 0)),
        ],
        out_shape=[jax.ShapeDtypeStruct((s, d), F32), jax.ShapeDtypeStruct((s, d), BF16)],
        compiler_params=_cparams(("parallel",)),
        name="proj_ln",
    )(oa, ob, w, w, x, g.reshape(1, d), b.reshape(1, d))


_FIRST, _LAST, _DIAG = 1, 2, 4


def _attn_kernel(qi_ref, ki_ref, bi_ref, fl_ref, *refs, tq, tk, n_sub, dq, slopes, bias_kind, diff,
                 lam_init):
    refs = list(refs)
    q_ref, k_ref, v_ref = refs[:3]
    pos = 3
    bias_ref = None
    if bias_kind is not None:
        bias_ref = refs[pos]
        pos += 1
    if diff:
        lam_ref, subg_ref = refs[pos], refs[pos + 1]
        pos += 2
    o_ref = refs[pos]
    m_sc, l_sc, acc_sc = refs[pos + 1:pos + 4]
    q2_sc = refs[pos + 4] if diff else None

    p = pl.program_id(0)
    fl = fl_ref[p]
    q0 = qi_ref[p] * tq
    k0 = ki_ref[p] * (tk * n_sub)
    rows = 2 * tq if diff else tq

    @pl.when((fl & _FIRST) != 0)
    def _():
        m_sc[...] = jnp.full_like(m_sc, NEG)
        l_sc[...] = jnp.zeros_like(l_sc)
        acc_sc[...] = jnp.zeros_like(acc_sc)
        if diff:
            lane = lax.broadcasted_iota(jnp.int32, (tq, HEAD_DIM), 1)
            for h in range(N_HEADS):
                qh = q_ref[:, h * HEAD_DIM:(h + 1) * HEAD_DIM]
                zero = jnp.zeros_like(qh)
                q2_sc[h, :tq, :] = jnp.where(lane < C_QK_DIM, qh, zero)
                q2_sc[h, tq:, :] = jnp.where(lane >= C_QK_DIM, qh, zero)

    def step(masked):
        for sub in range(n_sub):
            sub_step(masked, sub * tk)

    def sub_step(masked, off):
        kq = k0 + off - q0
        if slopes is not None:
            kp = kq.astype(F32) + lax.broadcasted_iota(jnp.int32, (1, tk), 1).astype(F32)
        if bias_ref is not None:
            if bias_kind == "table":
                bias = bias_ref[...].reshape(tq, tk)
            else:
                bias = bias_ref[:, off:off + tk].astype(F32)
        if masked:
            r = lax.broadcasted_iota(jnp.int32, (rows, tk), 0)
            if diff:
                r = jnp.where(r >= tq, r - tq, r)
            c = lax.broadcasted_iota(jnp.int32, (rows, tk), 1)
            keep = (r - c) >= kq
        ones = jnp.ones((tk, LANE), BF16)
        for h in range(N_HEADS):
            if diff:
                q = q2_sc[h]
            else:
                q = q_ref[:, h * dq:(h + 1) * dq]
            k = k_ref[off:off + tk, h * dq:(h + 1) * dq]
            s = lax.dot_general(q, k, (((1,), (1,)), ((), ())), preferred_element_type=F32)
            if slopes is not None:
                s = s + (slopes[h] * LOG2E) * kp
            if bias_ref is not None:
                s = s + bias
            if masked:
                s = jnp.where(keep, s, NEG)
            m_prev = m_sc[h]
            m_new = jnp.maximum(m_prev, jnp.max(s, axis=-1, keepdims=True))
            alpha = jnp.exp2(m_prev - m_new)
            pexp = jnp.exp2(s - jnp.concatenate([m_new] * (tk // LANE), axis=-1))
            v_aug = jnp.concatenate([v_ref[off:off + tk, h * HEAD_DIM:(h + 1) * HEAD_DIM], ones], axis=-1)
            pv = jnp.dot(pexp.astype(BF16), v_aug, preferred_element_type=F32)
            l_sc[h] = alpha * l_sc[h] + pv[:, HEAD_DIM:]
            acc_sc[h] = alpha * acc_sc[h] + pv[:, :HEAD_DIM]
            m_sc[h] = m_new

    if bias_kind is None:
        @pl.when((fl & _DIAG) != 0)
        def _():
            step(True)

        @pl.when((fl & _DIAG) == 0)
        def _():
            step(False)
    else:
        step(False)

    @pl.when((fl & _LAST) != 0)
    def _():
        if diff:
            lp = lam_ref[...]
            lam = (jnp.exp(jnp.sum(lp[0:1] * lp[1:2], axis=-1, keepdims=True))
                   - jnp.exp(jnp.sum(lp[2:3] * lp[3:4], axis=-1, keepdims=True)) + lam_init)
            for h in range(N_HEADS):
                o1 = acc_sc[h, :tq, :] / l_sc[h, :tq, :]
                o2 = acc_sc[h, tq:, :] / l_sc[h, tq:, :]
                o = o1 - lam * o2
                ms = jnp.mean(o * o, axis=-1, keepdims=True)
                o = o * lax.rsqrt(ms + DIFF_EPS) * subg_ref[...]
                o_ref[:, h * HEAD_DIM:(h + 1) * HEAD_DIM] = (o * (1.0 - lam_init)).astype(o_ref.dtype)
        else:
            for h in range(N_HEADS):
                o_ref[:, h * HEAD_DIM:(h + 1) * HEAD_DIM] = (acc_sc[h] / l_sc[h]).astype(o_ref.dtype)


def _causal_pairs(nq, ratio=1, lookback=None):
    qi, ki, bi, fl = [], [], [], []
    for q in range(nq):
        lo = 0 if lookback is None else max(0, q * ratio - lookback)
        hi = (q + 1) * ratio - 1
        for k in range(lo, hi + 1):
            qi.append(q)
            ki.append(k)
            bi.append(q * ratio - k)
            fl.append((_FIRST if k == lo else 0) | (_LAST if k == hi else 0)
                      | (_DIAG if k >= q * ratio else 0))
    return tuple(np.asarray(a, np.int32) for a in (qi, ki, bi, fl))


def _attention(q_arr, k_arr, v_arr, *, q_col, k_col, v_col, dq, slopes=None, bias=None, bias_kind=None,
               lookback=None, diff=False, lam=None, subg=None, lam_init=0.0, tq=512, tk=512, n_sub=1,
               name="attn"):
    s = q_arr.shape[0]
    tq = min(tq, s)
    tkb = tk * n_sub
    assert tq % tkb == 0 and s % tq == 0 and (bias_kind != "table" or (tq == tk and n_sub == 1))
    nq = s // tq
    qi, ki, bi, fl = _causal_pairs(nq, tq // tkb, lookback)
    qw = N_HEADS * (HEAD_DIM if diff else dq)
    kw = N_HEADS * dq
    in_specs = [
        pl.BlockSpec((tq, qw), lambda p, qi, ki, bi, fl: (qi[p], q_col)),
        pl.BlockSpec((tkb, kw), lambda p, qi, ki, bi, fl: (ki[p], k_col)),
        pl.BlockSpec((tkb, HW), lambda p, qi, ki, bi, fl: (ki[p], v_col)),
    ]
    args = [q_arr, k_arr, v_arr]
    if bias_kind == "table":
        in_specs.append(pl.BlockSpec((1, tq, tk), lambda p, qi, ki, bi, fl: (bi[p], 0, 0)))
        args.append(bias)
    elif bias_kind == "mask":
        in_specs.append(pl.BlockSpec((tq, tkb), lambda p, qi, ki, bi, fl: (qi[p], ki[p])))
        args.append(bias)
    rows = 2 * tq if diff else tq
    scratch = [pltpu.VMEM((N_HEADS, rows, LANE), F32), pltpu.VMEM((N_HEADS, rows, LANE), F32),
               pltpu.VMEM((N_HEADS, rows, HEAD_DIM), F32)]
    if diff:
        in_specs.append(pl.BlockSpec((4, C_QK_DIM), lambda p, qi, ki, bi, fl: (0, 0)))
        in_specs.append(pl.BlockSpec((1, HEAD_DIM), lambda p, qi, ki, bi, fl: (0, 0)))
        args += [lam, subg.reshape(1, HEAD_DIM)]
        scratch.append(pltpu.VMEM((N_HEADS, rows, HEAD_DIM), BF16))
    kern = functools.partial(_attn_kernel, tq=tq, tk=tk, n_sub=n_sub, dq=dq, slopes=slopes,
                             bias_kind=bias_kind, diff=diff, lam_init=lam_init)
    return pl.pallas_call(
        kern,
        grid_spec=pltpu.PrefetchScalarGridSpec(
            num_scalar_prefetch=4,
            grid=(len(qi),),
            in_specs=in_specs,
            out_specs=pl.BlockSpec((tq, HW), lambda p, qi, ki, bi, fl: (qi[p], 0)),
            scratch_shapes=scratch,
        ),
        out_shape=jax.ShapeDtypeStruct((s, HW), BF16),
        compiler_params=_cparams(("arbitrary",)),
        name=name,
    )(jnp.asarray(qi), jnp.asarray(ki), jnp.asarray(bi), jnp.asarray(fl), *args)


def _alibi_slopes(n):
    return tuple(2.0 ** (-8.0 * (h + 1) / n) for h in range(n))


def _dilated_bias_table(t):
    max_win = max(w for w, _ in A_PATTERNS)
    n_off = max_win // t + 1
    r = np.arange(t)[:, None]
    c = np.arange(t)[None, :]
    out = np.empty((n_off, t, t), np.float32)
    for o in range(n_off):
        dist = o * t + r - c
        mult = np.zeros((t, t), np.float64)
        for win, dil in A_PATTERNS:
            mult += (dist >= 0) & (dist <= win) & (dist % dil == 0)
        out[o] = np.where(mult > 0, np.log2(np.maximum(mult, 1.0)), NEG)
    return out


def _sortable_key(x):
    bits = pltpu.bitcast(x, jnp.int32)
    return jnp.where(bits < 0, bits ^ jnp.int32(0x7FFFFFFF), bits)


_CAND = 16


def _bitonic_sort_desc(x):
    x = list(x)
    n = len(x)
    k = 2
    while k <= n:
        j = k // 2
        while j >= 1:
            for i in range(n):
                l = i ^ j
                if l > i:
                    hi, lo = jnp.maximum(x[i], x[l]), jnp.minimum(x[i], x[l])
                    x[i], x[l] = (hi, lo) if (i & k) == 0 else (lo, hi)
            j //= 2
        k *= 2
    return x


def _merge_top_desc(a, b):
    n = len(a)
    c = [jnp.maximum(a[i], b[n - 1 - i]) for i in range(n)]
    j = n // 2
    while j >= 1:
        for i in range(n):
            l = i ^ j
            if l > i:
                c[i], c[l] = jnp.maximum(c[i], c[l]), jnp.minimum(c[i], c[l])
        j //= 2
    return c


def _indexer_kernel(q_ref, e1_ref, e2_ref, w_ref, o_ref, key_sc, cand_sc, thr_sc, nge_sc, *, tq, kc, topk,
                    wscale):
    qb = pl.program_id(0)
    q0 = qb * tq
    s_total = o_ref.shape[1]
    n_chunks = (q0 + tq + kc - 1) // kc
    w = w_ref[...] * wscale
    row = lax.broadcasted_iota(jnp.int32, (tq, kc), 0)
    col = lax.broadcasted_iota(jnp.int32, (tq, kc), 1)
    rel = row - col

    def score_chunk(c, carry):
        k0 = pl.multiple_of(c * kc, kc)
        k1 = e1_ref[pl.ds(k0, kc), :]
        k2 = e2_ref[pl.ds(k0, kc), :]
        acc = jnp.zeros((tq, kc), F32)
        for j in range(IDX_HEADS // 2):
            qp = q_ref[:, j * LANE:(j + 1) * LANE]
            s_even = lax.dot_general(qp, k1, (((1,), (1,)), ((), ())), preferred_element_type=F32)
            s_odd = lax.dot_general(qp, k2, (((1,), (1,)), ((), ())), preferred_element_type=F32)
            acc = acc + jnp.maximum(s_even, 0.0) * w[:, 2 * j:2 * j + 1]
            acc = acc + jnp.maximum(s_odd, 0.0) * w[:, 2 * j + 1:2 * j + 2]
        acc = jnp.where(acc == 0.0, 0.0, acc)
        key_sc[:, pl.ds(k0, kc)] = jnp.where(rel >= k0 - q0, acc, -jnp.inf)
        return carry

    lax.fori_loop(0, n_chunks, score_chunk, 0)

    int_min = jnp.int32(-2 ** 31)
    n_tiles = n_chunks * (kc // LANE)
    n_groups = (n_tiles + _CAND - 1) // _CAND

    def pad_tile(t, carry):
        key_sc[:, pl.ds(pl.multiple_of(t * LANE, LANE), LANE)] = jnp.full((tq, LANE), -jnp.inf, F32)
        return carry

    lax.fori_loop(n_tiles, n_groups * _CAND, pad_tile, 0)

    def row_group(r, carry):
        r0 = pl.multiple_of(r * 8, 8)

        def tile_group(g, top):
            base = g * (_CAND * LANE)
            x = [key_sc[pl.ds(r0, 8), pl.ds(pl.multiple_of(base + j * LANE, LANE), LANE)]
                 for j in range(_CAND)]
            return tuple(_merge_top_desc(list(top), _bitonic_sort_desc(x)))

        top0 = tuple(jnp.full((8, LANE), -jnp.inf, F32) for _ in range(_CAND))
        top = lax.fori_loop(0, n_groups, tile_group, top0)
        for j in range(_CAND):
            cand_sc[pl.ds(r0, 8), j * LANE:(j + 1) * LANE] = _sortable_key(top[j])
        return carry

    lax.fori_loop(0, tq // 8, row_group, 0)

    def count_cand(cand):
        cand_b = jnp.broadcast_to(cand, (tq, LANE))
        cnt = jnp.zeros((tq, LANE), jnp.int32)
        for j in range(_CAND):
            cnt = cnt + jnp.where(cand_sc[:, j * LANE:(j + 1) * LANE] >= cand_b, 1, 0)
        return jnp.sum(cnt, axis=-1, keepdims=True)

    def count_all(cand):
        cand_b = jnp.broadcast_to(cand, (tq, LANE))

        def body(t, cnt):
            blk = _sortable_key(key_sc[:, pl.ds(pl.multiple_of(t * LANE, LANE), LANE)])
            return cnt + jnp.where(blk >= cand_b, 1, 0)

        cnt = lax.fori_loop(0, n_tiles, body, jnp.zeros((tq, LANE), jnp.int32))
        return jnp.sum(cnt, axis=-1, keepdims=True)

    def kth_largest(count_ge, lo, hi):
        n_bits = jnp.max(32 - lax.clz(lo ^ hi))
        keep = jnp.where(n_bits >= 32, 0, -lax.shift_left(jnp.int32(1), jnp.minimum(n_bits, 31)))
        floor = jnp.where(n_bits >= 32, int_min, lo & keep)

        def bit_step(i, thr):
            cand = thr + lax.shift_left(jnp.int32(1), n_bits - 1 - i)
            return jnp.where(count_ge(cand) >= topk, cand, thr)

        return lax.fori_loop(0, n_bits, bit_step, floor)

    lo_tile = (topk + LANE - 1) // LANE - 1
    hi_tile = (topk - 1) // LANE
    lo = jnp.min(cand_sc[:, lo_tile * LANE:(lo_tile + 1) * LANE], axis=-1, keepdims=True)
    hi = jnp.max(cand_sc[:, hi_tile * LANE:(hi_tile + 1) * LANE], axis=-1, keepdims=True)
    thr = kth_largest(count_cand, lo, hi)
    thr_sc[...] = jnp.broadcast_to(thr, (tq, LANE))
    nge_sc[...] = jnp.broadcast_to(count_cand(thr), (tq, LANE))
    smallest_kept = cand_sc[:, (_CAND - 1) * LANE:]
    unsafe = jnp.max(jnp.where(smallest_kept >= thr_sc[...], 1, 0)) > 0

    @pl.when(jnp.logical_and(n_groups > 1, unsafe))
    def _():
        everything = (jnp.full((tq, 1), int_min, jnp.int32), jnp.full((tq, 1), 2 ** 31 - 1, jnp.int32))
        thr_all = kth_largest(count_all, *everything)
        thr_sc[...] = jnp.broadcast_to(thr_all, (tq, LANE))
        nge_sc[...] = jnp.broadcast_to(count_all(thr_all), (tq, LANE))

    thr_key = thr_sc[...]
    thr_b = pltpu.bitcast(jnp.where(thr_key < 0, thr_key ^ jnp.int32(0x7FFFFFFF), thr_key), F32)
    row_t = lax.broadcasted_iota(jnp.int32, (tq, LANE), 0)
    col_t = lax.broadcasted_iota(jnp.int32, (tq, LANE), 1)
    rel_t = row_t - col_t
    tied = jnp.max(jnp.where(nge_sc[...] > topk, 1, 0)) > 0

    @pl.when(jnp.logical_not(tied))
    def _():
        def mask_tile(t, carry):
            k0 = pl.multiple_of(t * LANE, LANE)
            sel = (key_sc[:, pl.ds(k0, LANE)] >= thr_b) & (rel_t >= k0 - q0)
            o_ref[:, pl.ds(k0, LANE)] = jnp.where(sel, 0.0, NEG).astype(o_ref.dtype)
            return carry

        lax.fori_loop(0, n_tiles, mask_tile, 0)

    @pl.when(tied)
    def _():
        need = topk - count_all(thr_key[:, :1] + 1)

        def count_tied_upto(pos_max):
            pos_b = jnp.broadcast_to(pos_max, (tq, LANE))

            def body(t, cnt):
                k0 = pl.multiple_of(t * LANE, LANE)
                blk = _sortable_key(key_sc[:, pl.ds(k0, LANE)])
                return cnt + jnp.where((blk == thr_key) & (col_t + k0 <= pos_b), 1, 0)

            cnt = lax.fori_loop(0, n_tiles, body, jnp.zeros((tq, LANE), jnp.int32))
            return jnp.sum(cnt, axis=-1, keepdims=True)

        n_pos_bits = max(1, (key_sc.shape[1] - 1).bit_length())

        def bit_step(i, cut):
            cand = cut + lax.shift_left(jnp.int32(1), n_pos_bits - 1 - i)
            return jnp.where(count_tied_upto(cand - 1) < need, cand, cut)

        cut = lax.fori_loop(0, n_pos_bits, bit_step, jnp.zeros((tq, 1), jnp.int32))
        cut_b = jnp.broadcast_to(cut, (tq, LANE))

        def mask_tile(t, carry):
            k0 = pl.multiple_of(t * LANE, LANE)
            blk = _sortable_key(key_sc[:, pl.ds(k0, LANE)])
            sel = (blk > thr_key) | ((blk == thr_key) & (col_t + k0 <= cut_b))
            sel = sel & (rel_t >= k0 - q0)
            o_ref[:, pl.ds(k0, LANE)] = jnp.where(sel, 0.0, NEG).astype(o_ref.dtype)
            return carry

        lax.fori_loop(0, n_tiles, mask_tile, 0)

    def fill_chunk(c, carry):
        k0 = pl.multiple_of(c * kc, kc)
        o_ref[:, pl.ds(k0, kc)] = jnp.full((tq, kc), NEG, o_ref.dtype)
        return carry

    lax.fori_loop(n_chunks, s_total // kc, fill_chunk, 0)


def _indexer_mask(proj, iw, *, q_col, e1_col, e2_col, topk, tq=256, kc=512):
    s = proj.shape[0]
    assert topk <= _CAND * LANE
    kern = functools.partial(_indexer_kernel, tq=tq, kc=kc, topk=topk,
                             wscale=(IDX_HEADS * IDX_DIM) ** -0.5)
    return pl.pallas_call(
        kern,
        grid=(s // tq,),
        in_specs=[
            pl.BlockSpec((tq, IDX_HEADS * IDX_DIM), lambda i: (i, q_col)),
            pl.BlockSpec((s, LANE), lambda i: (0, e1_col)),
            pl.BlockSpec((s, LANE), lambda i: (0, e2_col)),
            pl.BlockSpec((tq, LANE), lambda i: (i, 0)),
        ],
        out_specs=pl.BlockSpec((tq, s), lambda i: (i, 0)),
        out_shape=jax.ShapeDtypeStruct((s, s), BF16),
        scratch_shapes=[pltpu.VMEM((tq, -(-s // (_CAND * LANE)) * _CAND * LANE), F32),
                        pltpu.VMEM((tq, _CAND * LANE), jnp.int32),
                        pltpu.VMEM((tq, LANE), jnp.int32),
                        pltpu.VMEM((tq, LANE), jnp.int32)],
        compiler_params=_cparams(("parallel",)),
        name="indexer_topk",
    )(proj, proj, proj, iw)


def _rms_rows(x, g, eps):
    return x * lax.rsqrt(jnp.mean(x * x, axis=-1, keepdims=True) + eps) * g


def _mla_q_kernel(cq_ref, g_ref, wm_ref, wr_ref, cos_ref, sin_ref, o_ref, *, scale):
    n = _rms_rows(cq_ref[...], g_ref[...], RMS_EPS).astype(BF16)
    main = jnp.dot(n, wm_ref[...], preferred_element_type=F32)
    rot = jnp.dot(n, wr_ref[...], preferred_element_type=F32)
    cos, sin = cos_ref[...], sin_ref[...]
    for h in range(N_HEADS):
        base = h * 2 * LANE
        o_ref[:, base:base + LANE] = (main[:, base:base + LANE] * scale).astype(o_ref.dtype)
        pe = main[:, base + LANE:base + 2 * LANE] * cos + rot[:, h * LANE:(h + 1) * LANE] * sin
        o_ref[:, base + LANE:base + 2 * LANE] = (pe * scale).astype(o_ref.dtype)


def _mla_kv_kernel(ckv_ref, ef_ref, g_ref, w_ref, cos_ref, sin_ref, k_ref, v_ref):
    n = _rms_rows(ckv_ref[...], g_ref[...], RMS_EPS).astype(BF16)
    kv = jnp.dot(n, w_ref[...], preferred_element_type=F32)
    ef = ef_ref[...]
    kpe = (ef[:, :LANE] * cos_ref[...] + ef[:, LANE:] * sin_ref[...]).astype(k_ref.dtype)
    for h in range(N_HEADS):
        base = h * 2 * LANE
        k_ref[:, base:base + LANE] = kv[:, h * LANE:(h + 1) * LANE].astype(k_ref.dtype)
        k_ref[:, base + LANE:base + 2 * LANE] = kpe
    v_ref[...] = kv[:, HW:].astype(v_ref.dtype)


def _mla_prep(aux, gq, gkv, wq_main, wq_rot, wkv, cos_t, sin_t, *, tm=512):
    s = aux.shape[0]
    scale = (NOPE_DIM + ROPE_DIM) ** -0.5 * LOG2E
    q = pl.pallas_call(
        functools.partial(_mla_q_kernel, scale=scale),
        grid=(s // tm,),
        in_specs=[
            pl.BlockSpec((tm, Q_LORA), lambda i: (i, 1)),
            pl.BlockSpec((1, Q_LORA), lambda i: (0, 0)),
            pl.BlockSpec(wq_main.shape, lambda i: (0, 0)),
            pl.BlockSpec(wq_rot.shape, lambda i: (0, 0)),
            pl.BlockSpec((tm, LANE), lambda i: (i, 0)),
            pl.BlockSpec((tm, LANE), lambda i: (i, 0)),
        ],
        out_specs=pl.BlockSpec((tm, 2 * HW), lambda i: (i, 0)),
        out_shape=jax.ShapeDtypeStruct((s, 2 * HW), BF16),
        compiler_params=_cparams(("parallel",)),
        name="mla_q_prep",
    )(aux, gq.reshape(1, Q_LORA), wq_main, wq_rot, cos_t, sin_t)
    k, v = pl.pallas_call(
        _mla_kv_kernel,
        grid=(s // tm,),
        in_specs=[
            pl.BlockSpec((tm, KV_LORA), lambda i: (i, 0)),
            pl.BlockSpec((tm, 2 * LANE), lambda i: (i, 2)),
            pl.BlockSpec((1, KV_LORA), lambda i: (0, 0)),
            pl.BlockSpec(wkv.shape, lambda i: (0, 0)),
            pl.BlockSpec((tm, LANE), lambda i: (i, 0)),
            pl.BlockSpec((tm, LANE), lambda i: (i, 0)),
        ],
        out_specs=[pl.BlockSpec((tm, 2 * HW), lambda i: (i, 0)), pl.BlockSpec((tm, HW), lambda i: (i, 0))],
        out_shape=[jax.ShapeDtypeStruct((s, 2 * HW), BF16), jax.ShapeDtypeStruct((s, HW), BF16)],
        compiler_params=_cparams(("parallel",)),
        name="mla_kv_prep",
    )(aux, aux, gkv.reshape(1, KV_LORA), wkv, cos_t, sin_t)
    return q, k, v


def _rope_tables(s):
    half = ROPE_DIM // 2
    inv = ROPE_THETA ** (-jnp.arange(half, dtype=F32) / half)
    ang = jnp.arange(s, dtype=jnp.int32).astype(F32)[:, None] * inv[None, :]
    cos, sin = jnp.cos(ang), jnp.sin(ang)
    pad = jnp.zeros((s, LANE - ROPE_DIM), F32)
    return (jnp.concatenate([cos, cos, pad], -1), jnp.concatenate([-sin, sin, pad], -1))


def _swap_halves(w):
    half = w.shape[-1] // 2
    return jnp.concatenate([w[..., half:], w[..., :half]], -1)


def _even_mixer(y, yb, w_in, w_out, g, b):
    s = y.shape[0]
    d = w_in.shape[0]
    att_scale = HEAD_DIM ** -0.5 * LOG2E
    n_qkv = 6 * HW
    iq_w = w_in[:, n_qkv:n_qkv + IDX_HEADS * IDX_DIM]
    ik_w = w_in[:, n_qkv + IDX_HEADS * IDX_DIM:n_qkv + IDX_HEADS * IDX_DIM + IDX_DIM]
    iw_w = w_in[:, n_qkv + IDX_HEADS * IDX_DIM + IDX_DIM:]
    z64 = jnp.zeros((d, LANE - IDX_DIM), F32)
    w_main = jnp.concatenate([w_in[:, :n_qkv], iq_w, ik_w, z64, z64, ik_w, jnp.zeros((d, 2 * LANE), F32)],
                             -1).astype(BF16)
    ones = jnp.ones((HW,), F32)
    colscale = jnp.concatenate([ones * att_scale, ones, ones, ones * att_scale, ones, ones, ones,
                                jnp.ones((4 * LANE,), F32)])
    proj = _matmul(yb, w_main, colscale, BF16, name="even_in_proj")
    w_iw = jnp.concatenate([iw_w, jnp.zeros((d, LANE - IDX_HEADS), F32)], -1).astype(BF16)
    iw = _matmul(yb, w_iw, jnp.ones((LANE,), F32), F32, tn=LANE, name="even_iw_proj")

    slopes = _alibi_slopes(N_HEADS)
    t = 512
    table = jnp.asarray(_dilated_bias_table(t))
    lookback = max(w for w, _ in A_PATTERNS) // t
    oa = _attention(proj, proj, proj, q_col=0, k_col=1, v_col=2, dq=HEAD_DIM, slopes=slopes,
                    bias=table, bias_kind="table", lookback=lookback, tq=t, tk=t, name="dilated_attn")

    topk = min(B_TOPK_MAX, s // 4)
    mask = _indexer_mask(proj, iw, q_col=6, e1_col=7 * HW // LANE, e2_col=7 * HW // LANE + 1, topk=topk)
    ob = _attention(proj, proj, proj, q_col=3, k_col=4, v_col=5, dq=HEAD_DIM, slopes=slopes,
                    bias=mask, bias_kind="mask", tq=ATT_TQ, tk=ATT_TK, n_sub=ATT_NSUB, name="dsa_attn")
    return _proj_ln(oa, ob, w_out.astype(BF16), y, g, b)


def _odd_mixer(y, yb, w_in, w_out, lam_params, subln_g, gq, gkv, w_uq, w_ukv, lam_init, g, b):
    s = y.shape[0]
    d = w_in.shape[0]
    w_c = w_in[:, :3 * HW].astype(BF16)
    c_scale = jnp.concatenate([jnp.full((HW,), C_QK_DIM ** -0.5 * LOG2E, F32), jnp.ones((2 * HW,), F32)])
    proj = _matmul(yb, w_c, c_scale, BF16, name="odd_in_proj")
    mq_w = w_in[:, 3 * HW:3 * HW + Q_LORA]
    mkv_w = w_in[:, 3 * HW + Q_LORA:3 * HW + Q_LORA + KV_LORA]
    kpe_w = w_in[:, 3 * HW + Q_LORA + KV_LORA:]
    z64 = jnp.zeros((d, LANE - ROPE_DIM), F32)
    w_aux = jnp.concatenate([mkv_w, kpe_w, z64, _swap_halves(kpe_w), z64, mq_w], -1).astype(BF16)
    aux = _matmul(yb, w_aux, jnp.ones((w_aux.shape[1],), F32), F32, name="odd_aux_proj")

    wq = w_uq.reshape(Q_LORA, N_HEADS, NOPE_DIM + ROPE_DIM)
    zq = jnp.zeros((Q_LORA, N_HEADS, LANE - ROPE_DIM), F32)
    wq_main = jnp.concatenate([wq, zq], -1).reshape(Q_LORA, 2 * HW).astype(BF16)
    wq_rot = jnp.concatenate([_swap_halves(wq[..., NOPE_DIM:]), zq], -1).reshape(Q_LORA, HW).astype(BF16)
    wkv = w_ukv.reshape(KV_LORA, N_HEADS, NOPE_DIM + V_DIM)
    wkv = jnp.concatenate([wkv[..., :NOPE_DIM].reshape(KV_LORA, HW),
                           wkv[..., NOPE_DIM:].reshape(KV_LORA, HW)], -1).astype(BF16)
    cos_t, sin_t = _rope_tables(s)
    mq, mk, mv = _mla_prep(aux, gq, gkv, wq_main, wq_rot, wkv, cos_t, sin_t)

    oc = _attention(proj, proj, proj, q_col=0, k_col=1, v_col=2, dq=HEAD_DIM,
                    slopes=_alibi_slopes(N_HEADS), diff=True, lam=lam_params, subg=subln_g,
                    lam_init=lam_init, tq=ATT_TQ, tk=ATT_TK, name="diff_attn")
    od = _attention(mq, mk, mv, q_col=0, k_col=0, v_col=0, dq=2 * LANE, tq=ATT_TQ, tk=ATT_TK,
                    n_sub=ATT_NSUB, name="mla_attn")
    return _proj_ln(oc, od, w_out.astype(BF16), y, g, b)


def kernel(x, ffn_w_gate, ffn_w_up, ffn_w_down, ln_g, ln_b, ev_w_in, ev_w_out, od_w_in, od_w_out,
           od_lambda, od_subln_g, od_q_norm_g, od_kv_norm_g, od_w_uq, od_w_ukv):
    batch, seq, d_model = x.shape
    xs = x.reshape(batch * seq, d_model)
    wg, wu, wd = (w.astype(BF16) for w in (ffn_w_gate, ffn_w_up, ffn_w_down))
    outs = []
    for bi in range(batch):
        y = xs if batch == 1 else xs[bi * seq:(bi + 1) * seq]
        yb = y.astype(BF16)
        for i in range(DEPTH):
            j = i // 2
            y, yb = _ffn_ln(y, yb, wg, wu, wd, ln_g[i, 0], ln_b[i, 0], layer=i, sub=0)
            if i % 2 == 0:
                y, yb = _even_mixer(y, yb, ev_w_in[j], ev_w_out[j], ln_g[i, 1], ln_b[i, 1])
            else:
                lam_init = 0.8 - 0.6 * math.exp(-0.3 * i)
                y, yb = _odd_mixer(y, yb, od_w_in[j], od_w_out[j], od_lambda[j], od_subln_g[j],
                                   od_q_norm_g[j], od_kv_norm_g[j], od_w_uq[j], od_w_ukv[j], lam_init,
                                   ln_g[i, 1], ln_b[i, 1])
            y, yb = _ffn_ln(y, yb, wg, wu, wd, ln_g[i, 2], ln_b[i, 2], layer=i, sub=1)
        outs.append(y)
    out = outs[0] if batch == 1 else jnp.concatenate(outs, 0)
    return out.reshape(batch, seq, d_model)
```

```python
import functools
import math

import numpy as np
import jax
import jax.numpy as jnp
from jax import lax
from jax.experimental import pallas as pl
from jax.experimental.pallas import tpu as pltpu

F32 = jnp.float32
BF16 = jnp.bfloat16

D_MODEL = 2048
DEPTH = 2
HEAD_DIM = 128
N_HEADS = 8
A_PATTERNS = ((128, 1), (512, 4), (2048, 16))
IDX_HEADS = 16
IDX_DIM = 64
B_TOPK_MAX = 256
C_QK_DIM = 64
Q_LORA = 768
KV_LORA = 512
NOPE_DIM = 128
ROPE_DIM = 64
V_DIM = 128
ROPE_THETA = 10000.0
D_FF = 5632
LN_EPS = 1e-5
RMS_EPS = 1e-6
DIFF_EPS = 1e-5
ALPHA = (2 * DEPTH) ** 0.25

LANE = 128
NEG = -1e30
LOG2E = math.log2(math.e)
VMEM_LIMIT = 56 * 1024 * 1024

HW = N_HEADS * HEAD_DIM

ATT_TQ = 1024
ATT_TK = 512
ATT_NSUB = 2
DIL_T = 512
FFN_TM, FFN_TF = 512, 512
PROJ_TM, PROJ_TN = 1024, 512
ROW_TM = 512
IDX_TQ, IDX_KC = 256, 512
SEL_ROWS = 16


def _cparams(sem):
    return pltpu.CompilerParams(dimension_semantics=sem, vmem_limit_bytes=VMEM_LIMIT)


def _layernorm_rows(z, g, b):
    mu = jnp.mean(z, axis=-1, keepdims=True)
    zc = z - mu
    var = jnp.mean(zc * zc, axis=-1, keepdims=True)
    return zc * lax.rsqrt(var + LN_EPS) * g + b


def _ffn_ln_kernel(xb_ref, x_ref, wg_ref, wu_ref, wd_ref, g_ref, b_ref, y_ref, yb_ref, acc_sc):
    j = pl.program_id(1)

    @pl.when(j == 0)
    def _():
        acc_sc[...] = jnp.zeros_like(acc_sc)

    xb = xb_ref[...]
    gate = jnp.dot(xb, wg_ref[...], preferred_element_type=F32)
    up = jnp.dot(xb, wu_ref[...], preferred_element_type=F32)
    h = (gate * jax.nn.sigmoid(gate)) * up
    acc_sc[...] += jnp.dot(h.astype(BF16), wd_ref[...], preferred_element_type=F32)

    @pl.when(j == pl.num_programs(1) - 1)
    def _():
        z = ALPHA * x_ref[...] + 0.5 * acc_sc[...]
        y = _layernorm_rows(z, g_ref[...], b_ref[...])
        y_ref[...] = y
        yb_ref[...] = y.astype(BF16)


def _ffn_ln(x, xb, wg, wu, wd, g, b, *, layer, sub, tm=FFN_TM, tf=FFN_TF):
    s, d = x.shape
    dff = wg.shape[-1]
    row = lambda i, j: (i, 0)
    return pl.pallas_call(
        _ffn_ln_kernel,
        grid=(s // tm, dff // tf),
        in_specs=[
            pl.BlockSpec((tm, d), row),
            pl.BlockSpec((tm, d), row),
            pl.BlockSpec((None, None, d, tf), lambda i, j: (layer, sub, 0, j)),
            pl.BlockSpec((None, None, d, tf), lambda i, j: (layer, sub, 0, j)),
            pl.BlockSpec((None, None, tf, d), lambda i, j: (layer, sub, j, 0)),
            pl.BlockSpec((1, d), lambda i, j: (0, 0)),
            pl.BlockSpec((1, d), lambda i, j: (0, 0)),
        ],
        out_specs=[pl.BlockSpec((tm, d), row), pl.BlockSpec((tm, d), row)],
        out_shape=[jax.ShapeDtypeStruct((s, d), F32), jax.ShapeDtypeStruct((s, d), BF16)],
        scratch_shapes=[pltpu.VMEM((tm, d), F32)],
        compiler_params=_cparams(("parallel", "arbitrary")),
        name="ffn_ln",
    )(xb, x, wg, wu, wd, g.reshape(1, d), b.reshape(1, d))


def _mm_kernel(x_ref, w_ref, s_ref, o_ref):
    acc = jnp.dot(x_ref[...], w_ref[...], preferred_element_type=F32)
    o_ref[...] = (acc * s_ref[...]).astype(o_ref.dtype)


def _matmul(x, w, colscale, out_dtype, *, tm=PROJ_TM, tn=PROJ_TN, name="proj"):
    m, k = x.shape
    n = w.shape[1]
    tm, tn = min(tm, m), min(tn, n)
    assert m % tm == 0 and n % tn == 0
    return pl.pallas_call(
        _mm_kernel,
        grid=(m // tm, n // tn),
        in_specs=[
            pl.BlockSpec((tm, k), lambda i, j: (i, 0)),
            pl.BlockSpec((k, tn), lambda i, j: (0, j)),
            pl.BlockSpec((1, tn), lambda i, j: (0, j)),
        ],
        out_specs=pl.BlockSpec((tm, tn), lambda i, j: (i, j)),
        out_shape=jax.ShapeDtypeStruct((m, n), out_dtype),
        compiler_params=_cparams(("parallel", "arbitrary")),
        name=name,
    )(x, w, colscale.reshape(1, n))


def _proj_ln_kernel(oa_ref, ob_ref, wa_ref, wb_ref, x_ref, g_ref, b_ref, y_ref, yb_ref):
    mix = jnp.dot(oa_ref[...], wa_ref[...], preferred_element_type=F32)
    mix += jnp.dot(ob_ref[...], wb_ref[...], preferred_element_type=F32)
    y = _layernorm_rows(ALPHA * x_ref[...] + mix, g_ref[...], b_ref[...])
    y_ref[...] = y
    yb_ref[...] = y.astype(BF16)


def _proj_ln(oa, ob, w, x, g, b, *, tm=ROW_TM):
    s, d = x.shape
    ka, kb = oa.shape[1], ob.shape[1]
    assert ka == kb and w.shape[0] == ka + kb
    return pl.pallas_call(
        _proj_ln_kernel,
        grid=(s // tm,),
        in_specs=[
            pl.BlockSpec((tm, ka), lambda i: (i, 0)),
            pl.BlockSpec((tm, kb), lambda i: (i, 0)),
            pl.BlockSpec((ka, d), lambda i: (0, 0)),
            pl.BlockSpec((kb, d), lambda i: (1, 0)),
            pl.BlockSpec((tm, d), lambda i: (i, 0)),
            pl.BlockSpec((1, d), lambda i: (0, 0)),
            pl.BlockSpec((1, d), lambda i: (0, 0)),
        ],
        out_specs=[
            pl.BlockSpec((tm, d), lambda i: (i, 0)),
            pl.BlockSpec((tm, d), lambda i: (i, 0)),
        ],
        out_shape=[jax.ShapeDtypeStruct((s, d), F32), jax.ShapeDtypeStruct((s, d), BF16)],
        compiler_params=_cparams(("parallel",)),
        name="proj_ln",
    )(oa, ob, w, w, x, g.reshape(1, d), b.reshape(1, d))


_FIRST, _LAST, _DIAG = 1, 2, 4


def _attn_kernel(qi_ref, ki_ref, bi_ref, fl_ref, *refs, tq, tk, n_sub, dq, slopes, bias_kind, diff,
                 lam_init):
    refs = list(refs)
    q_ref, k_ref, v_ref = refs[:3]
    pos = 3
    bias_ref = None
    if bias_kind is not None:
        bias_ref = refs[pos]
        pos += 1
    if diff:
        lam_ref, subg_ref = refs[pos], refs[pos + 1]
        pos += 2
    o_ref = refs[pos]
    m_sc, l_sc, acc_sc = refs[pos + 1:pos + 4]
    q2_sc = refs[pos + 4] if diff else None

    p = pl.program_id(0)
    fl = fl_ref[p]
    q0 = qi_ref[p] * tq
    k0 = ki_ref[p] * (tk * n_sub)
    rows = 2 * tq if diff else tq

    @pl.when((fl & _FIRST) != 0)
    def _():
        m_sc[...] = jnp.full_like(m_sc, NEG)
        l_sc[...] = jnp.zeros_like(l_sc)
        acc_sc[...] = jnp.zeros_like(acc_sc)
        if diff:
            lane = lax.broadcasted_iota(jnp.int32, (tq, HEAD_DIM), 1)
            for h in range(N_HEADS):
                qh = q_ref[:, h * HEAD_DIM:(h + 1) * HEAD_DIM]
                zero = jnp.zeros_like(qh)
                q2_sc[h, :tq, :] = jnp.where(lane < C_QK_DIM, qh, zero)
                q2_sc[h, tq:, :] = jnp.where(lane >= C_QK_DIM, qh, zero)

    def step(masked):
        for sub in range(n_sub):
            sub_step(masked, sub * tk)

    def sub_step(masked, off):
        kq = k0 + off - q0
        if slopes is not None:
            kp = kq.astype(F32) + lax.broadcasted_iota(jnp.int32, (1, tk), 1).astype(F32)
        if bias_ref is not None:
            if bias_kind == "table":
                bias = bias_ref[...].reshape(tq, tk)
            else:
                bias = bias_ref[:, off:off + tk].astype(F32)
        if masked:
            r = lax.broadcasted_iota(jnp.int32, (rows, tk), 0)
            if diff:
                r = jnp.where(r >= tq, r - tq, r)
            c = lax.broadcasted_iota(jnp.int32, (rows, tk), 1)
            keep = (r - c) >= kq
        ones = jnp.ones((tk, LANE), BF16)
        for h in range(N_HEADS):
            if diff:
                q = q2_sc[h]
            else:
                q = q_ref[:, h * dq:(h + 1) * dq]
            k = k_ref[off:off + tk, h * dq:(h + 1) * dq]
            s = lax.dot_general(q, k, (((1,), (1,)), ((), ())), preferred_element_type=F32)
            if slopes is not None:
                s = s + (slopes[h] * LOG2E) * kp
            if bias_ref is not None:
                s = s + bias
            if masked:
                s = jnp.where(keep, s, NEG)
            m_prev = m_sc[h]
            m_new = jnp.maximum(m_prev, jnp.max(s, axis=-1, keepdims=True))
            alpha = jnp.exp2(m_prev - m_new)
            pexp = jnp.exp2(s - jnp.concatenate([m_new] * (tk // LANE), axis=-1))
            v_aug = jnp.concatenate([v_ref[off:off + tk, h * HEAD_DIM:(h + 1) * HEAD_DIM], ones], axis=-1)
            pv = jnp.dot(pexp.astype(BF16), v_aug, preferred_element_type=F32)
            l_sc[h] = alpha * l_sc[h] + pv[:, HEAD_DIM:]
            acc_sc[h] = alpha * acc_sc[h] + pv[:, :HEAD_DIM]
            m_sc[h] = m_new

    if bias_kind is None:
        @pl.when((fl & _DIAG) != 0)
        def _():
            step(True)

        @pl.when((fl & _DIAG) == 0)
        def _():
            step(False)
    else:
        step(False)

    @pl.when((fl & _LAST) != 0)
    def _():
        if diff:
            lp = lam_ref[...]
            lam = (jnp.exp(jnp.sum(lp[0:1] * lp[1:2], axis=-1, keepdims=True))
                   - jnp.exp(jnp.sum(lp[2:3] * lp[3:4], axis=-1, keepdims=True)) + lam_init)
            for h in range(N_HEADS):
                o1 = acc_sc[h, :tq, :] / l_sc[h, :tq, :]
                o2 = acc_sc[h, tq:, :] / l_sc[h, tq:, :]
                o = o1 - lam * o2
                ms = jnp.mean(o * o, axis=-1, keepdims=True)
                o = o * lax.rsqrt(ms + DIFF_EPS) * subg_ref[...]
                o_ref[:, h * HEAD_DIM:(h + 1) * HEAD_DIM] = (o * (1.0 - lam_init)).astype(o_ref.dtype)
        else:
            for h in range(N_HEADS):
                o_ref[:, h * HEAD_DIM:(h + 1) * HEAD_DIM] = (acc_sc[h] / l_sc[h]).astype(o_ref.dtype)


def _causal_pairs(nq, ratio=1, lookback=None):
    qi, ki, bi, fl = [], [], [], []
    for q in range(nq):
        lo = 0 if lookback is None else max(0, q * ratio - lookback)
        hi = (q + 1) * ratio - 1
        for k in range(lo, hi + 1):
            qi.append(q)
            ki.append(k)
            bi.append(q * ratio - k)
            fl.append((_FIRST if k == lo else 0) | (_LAST if k == hi else 0)
                      | (_DIAG if k >= q * ratio else 0))
    return tuple(np.asarray(a, np.int32) for a in (qi, ki, bi, fl))


def _attention(q_arr, k_arr, v_arr, *, q_col, k_col, v_col, dq, slopes=None, bias=None, bias_kind=None,
               lookback=None, diff=False, lam=None, subg=None, lam_init=0.0, tq=512, tk=512, n_sub=1,
               name="attn"):
    s = q_arr.shape[0]
    tq = min(tq, s)
    tkb = tk * n_sub
    assert tq % tkb == 0 and s % tq == 0 and (bias_kind != "table" or (tq == tk and n_sub == 1))
    nq = s // tq
    qi, ki, bi, fl = _causal_pairs(nq, tq // tkb, lookback)
    qw = N_HEADS * (HEAD_DIM if diff else dq)
    kw = N_HEADS * dq
    in_specs = [
        pl.BlockSpec((tq, qw), lambda p, qi, ki, bi, fl: (qi[p], q_col)),
        pl.BlockSpec((tkb, kw), lambda p, qi, ki, bi, fl: (ki[p], k_col)),
        pl.BlockSpec((tkb, HW), lambda p, qi, ki, bi, fl: (ki[p], v_col)),
    ]
    args = [q_arr, k_arr, v_arr]
    if bias_kind == "table":
        in_specs.append(pl.BlockSpec((1, tq, tk), lambda p, qi, ki, bi, fl: (bi[p], 0, 0)))
        args.append(bias)
    elif bias_kind == "mask":
        in_specs.append(pl.BlockSpec((tq, tkb), lambda p, qi, ki, bi, fl: (qi[p], ki[p])))
        args.append(bias)
    rows = 2 * tq if diff else tq
    scratch = [pltpu.VMEM((N_HEADS, rows, LANE), F32), pltpu.VMEM((N_HEADS, rows, LANE), F32),
               pltpu.VMEM((N_HEADS, rows, HEAD_DIM), F32)]
    if diff:
        in_specs.append(pl.BlockSpec((4, C_QK_DIM), lambda p, qi, ki, bi, fl: (0, 0)))
        in_specs.append(pl.BlockSpec((1, HEAD_DIM), lambda p, qi, ki, bi, fl: (0, 0)))
        args += [lam, subg.reshape(1, HEAD_DIM)]
        scratch.append(pltpu.VMEM((N_HEADS, rows, HEAD_DIM), BF16))
    kern = functools.partial(_attn_kernel, tq=tq, tk=tk, n_sub=n_sub, dq=dq, slopes=slopes,
                             bias_kind=bias_kind, diff=diff, lam_init=lam_init)
    return pl.pallas_call(
        kern,
        grid_spec=pltpu.PrefetchScalarGridSpec(
            num_scalar_prefetch=4,
            grid=(len(qi),),
            in_specs=in_specs,
            out_specs=pl.BlockSpec((tq, HW), lambda p, qi, ki, bi, fl: (qi[p], 0)),
            scratch_shapes=scratch,
        ),
        out_shape=jax.ShapeDtypeStruct((s, HW), BF16),
        compiler_params=_cparams(("arbitrary",)),
        name=name,
    )(jnp.asarray(qi), jnp.asarray(ki), jnp.asarray(bi), jnp.asarray(fl), *args)


def _alibi_slopes(n):
    return tuple(2.0 ** (-8.0 * (h + 1) / n) for h in range(n))


def _dilated_bias_table(t):
    max_win = max(w for w, _ in A_PATTERNS)
    n_off = max_win // t + 1
    r = np.arange(t)[:, None]
    c = np.arange(t)[None, :]
    out = np.empty((n_off, t, t), np.float32)
    for o in range(n_off):
        dist = o * t + r - c
        mult = np.zeros((t, t), np.float64)
        for win, dil in A_PATTERNS:
            mult += (dist >= 0) & (dist <= win) & (dist % dil == 0)
        out[o] = np.where(mult > 0, np.log2(np.maximum(mult, 1.0)), NEG)
    return out


def _sortable_key(x):
    bits = pltpu.bitcast(x, jnp.int32)
    return jnp.where(bits < 0, bits ^ jnp.int32(0x7FFFFFFF), bits)


_CAND = 16


def _bitonic_sort_desc(x):
    x = list(x)
    n = len(x)
    k = 2
    while k <= n:
        j = k // 2
        while j >= 1:
            for i in range(n):
                l = i ^ j
                if l > i:
                    hi, lo = jnp.maximum(x[i], x[l]), jnp.minimum(x[i], x[l])
                    x[i], x[l] = (hi, lo) if (i & k) == 0 else (lo, hi)
            j //= 2
        k *= 2
    return x


def _merge_top_desc(a, b):
    n = len(a)
    c = [jnp.maximum(a[i], b[n - 1 - i]) for i in range(n)]
    j = n // 2
    while j >= 1:
        for i in range(n):
            l = i ^ j
            if l > i:
                c[i], c[l] = jnp.maximum(c[i], c[l]), jnp.minimum(c[i], c[l])
        j //= 2
    return c


def _indexer_kernel(q_ref, e1_ref, e2_ref, w_ref, o_ref, key_sc, cand_sc, thr_sc, nge_sc, *, tq, kc, topk,
                    wscale):
    qb = pl.program_id(0)
    q0 = qb * tq
    s_total = o_ref.shape[1]
    n_chunks = (q0 + tq + kc - 1) // kc
    w = w_ref[...] * wscale
    row = lax.broadcasted_iota(jnp.int32, (tq, kc), 0)
    col = lax.broadcasted_iota(jnp.int32, (tq, kc), 1)
    rel = row - col

    def score_chunk(c, carry):
        k0 = pl.multiple_of(c * kc, kc)
        k1 = e1_ref[pl.ds(k0, kc), :]
        k2 = e2_ref[pl.ds(k0, kc), :]
        acc = jnp.zeros((tq, kc), F32)
        for j in range(IDX_HEADS // 2):
            qp = q_ref[:, j * LANE:(j + 1) * LANE]
            s_even = lax.dot_general(qp, k1, (((1,), (1,)), ((), ())), preferred_element_type=F32)
            s_odd = lax.dot_general(qp, k2, (((1,), (1,)), ((), ())), preferred_element_type=F32)
            acc = acc + jnp.maximum(s_even, 0.0) * w[:, 2 * j:2 * j + 1]
            acc = acc + jnp.maximum(s_odd, 0.0) * w[:, 2 * j + 1:2 * j + 2]
        acc = jnp.where(acc == 0.0, 0.0, acc)
        key_sc[:, pl.ds(k0, kc)] = jnp.where(rel >= k0 - q0, acc, -jnp.inf)
        return carry

    lax.fori_loop(0, n_chunks, score_chunk, 0)

    int_min = jnp.int32(-2 ** 31)
    n_tiles = n_chunks * (kc // LANE)
    n_groups = (n_tiles + _CAND - 1) // _CAND

    def pad_tile(t, carry):
        key_sc[:, pl.ds(pl.multiple_of(t * LANE, LANE), LANE)] = jnp.full((tq, LANE), -jnp.inf, F32)
        return carry

    lax.fori_loop(n_tiles, n_groups * _CAND, pad_tile, 0)

    def row_group(r, carry):
        r0 = pl.multiple_of(r * SEL_ROWS, SEL_ROWS)

        def tile_group(g, top):
            base = g * (_CAND * LANE)
            x = [key_sc[pl.ds(r0, SEL_ROWS), pl.ds(pl.multiple_of(base + j * LANE, LANE), LANE)]
                 for j in range(_CAND)]
            return tuple(_merge_top_desc(list(top), _bitonic_sort_desc(x)))

        top0 = tuple(jnp.full((SEL_ROWS, LANE), -jnp.inf, F32) for _ in range(_CAND))
        top = lax.fori_loop(0, n_groups, tile_group, top0)
        for j in range(_CAND):
            cand_sc[pl.ds(r0, SEL_ROWS), j * LANE:(j + 1) * LANE] = _sortable_key(top[j])
        return carry

    lax.fori_loop(0, tq // SEL_ROWS, row_group, 0)

    def count_cand(cand):
        cand_b = jnp.broadcast_to(cand, (tq, LANE))
        cnt = jnp.zeros((tq, LANE), jnp.int32)
        for j in range(_CAND):
            cnt = cnt + jnp.where(cand_sc[:, j * LANE:(j + 1) * LANE] >= cand_b, 1, 0)
        return jnp.sum(cnt, axis=-1, keepdims=True)

    def count_all(cand):
        cand_b = jnp.broadcast_to(cand, (tq, LANE))

        def body(t, cnt):
            blk = _sortable_key(key_sc[:, pl.ds(pl.multiple_of(t * LANE, LANE), LANE)])
            return cnt + jnp.where(blk >= cand_b, 1, 0)

        cnt = lax.fori_loop(0, n_tiles, body, jnp.zeros((tq, LANE), jnp.int32))
        return jnp.sum(cnt, axis=-1, keepdims=True)

    def kth_largest(count_ge, lo, hi):
        n_bits = jnp.max(32 - lax.clz(lo ^ hi))
        keep = jnp.where(n_bits >= 32, 0, -lax.shift_left(jnp.int32(1), jnp.minimum(n_bits, 31)))
        floor = jnp.where(n_bits >= 32, int_min, lo & keep)

        def bit_step(i, thr):
            cand = thr + lax.shift_left(jnp.int32(1), n_bits - 1 - i)
            return jnp.where(count_ge(cand) >= topk, cand, thr)

        return lax.fori_loop(0, n_bits, bit_step, floor)

    lo_tile = (topk + LANE - 1) // LANE - 1
    hi_tile = (topk - 1) // LANE
    lo = jnp.min(cand_sc[:, lo_tile * LANE:(lo_tile + 1) * LANE], axis=-1, keepdims=True)
    hi = jnp.max(cand_sc[:, hi_tile * LANE:(hi_tile + 1) * LANE], axis=-1, keepdims=True)
    thr = kth_largest(count_cand, lo, hi)
    thr_sc[...] = jnp.broadcast_to(thr, (tq, LANE))
    nge_sc[...] = jnp.broadcast_to(count_cand(thr), (tq, LANE))
    smallest_kept = cand_sc[:, (_CAND - 1) * LANE:]
    unsafe = jnp.max(jnp.where(smallest_kept >= thr_sc[...], 1, 0)) > 0

    @pl.when(jnp.logical_and(n_groups > 1, unsafe))
    def _():
        everything = (jnp.full((tq, 1), int_min, jnp.int32), jnp.full((tq, 1), 2 ** 31 - 1, jnp.int32))
        thr_all = kth_largest(count_all, *everything)
        thr_sc[...] = jnp.broadcast_to(thr_all, (tq, LANE))
        nge_sc[...] = jnp.broadcast_to(count_all(thr_all), (tq, LANE))

    thr_key = thr_sc[...]
    thr_b = pltpu.bitcast(jnp.where(thr_key < 0, thr_key ^ jnp.int32(0x7FFFFFFF), thr_key), F32)
    row_t = lax.broadcasted_iota(jnp.int32, (tq, LANE), 0)
    col_t = lax.broadcasted_iota(jnp.int32, (tq, LANE), 1)
    rel_t = row_t - col_t
    tied = jnp.max(jnp.where(nge_sc[...] > topk, 1, 0)) > 0

    @pl.when(jnp.logical_not(tied))
    def _():
        def mask_tile(t, carry):
            k0 = pl.multiple_of(t * LANE, LANE)
            sel = (key_sc[:, pl.ds(k0, LANE)] >= thr_b) & (rel_t >= k0 - q0)
            o_ref[:, pl.ds(k0, LANE)] = jnp.where(sel, 0.0, NEG).astype(o_ref.dtype)
            return carry

        lax.fori_loop(0, n_tiles, mask_tile, 0)

    @pl.when(tied)
    def _():
        need = topk - count_all(thr_key[:, :1] + 1)

        def count_tied_upto(pos_max):
            pos_b = jnp.broadcast_to(pos_max, (tq, LANE))

            def body(t, cnt):
                k0 = pl.multiple_of(t * LANE, LANE)
                blk = _sortable_key(key_sc[:, pl.ds(k0, LANE)])
                return cnt + jnp.where((blk == thr_key) & (col_t + k0 <= pos_b), 1, 0)

            cnt = lax.fori_loop(0, n_tiles, body, jnp.zeros((tq, LANE), jnp.int32))
            return jnp.sum(cnt, axis=-1, keepdims=True)

        n_pos_bits = max(1, (key_sc.shape[1] - 1).bit_length())

        def bit_step(i, cut):
            cand = cut + lax.shift_left(jnp.int32(1), n_pos_bits - 1 - i)
            return jnp.where(count_tied_upto(cand - 1) < need, cand, cut)

        cut = lax.fori_loop(0, n_pos_bits, bit_step, jnp.zeros((tq, 1), jnp.int32))
        cut_b = jnp.broadcast_to(cut, (tq, LANE))

        def mask_tile(t, carry):
            k0 = pl.multiple_of(t * LANE, LANE)
            blk = _sortable_key(key_sc[:, pl.ds(k0, LANE)])
            sel = (blk > thr_key) | ((blk == thr_key) & (col_t + k0 <= cut_b))
            sel = sel & (rel_t >= k0 - q0)
            o_ref[:, pl.ds(k0, LANE)] = jnp.where(sel, 0.0, NEG).astype(o_ref.dtype)
            return carry

        lax.fori_loop(0, n_tiles, mask_tile, 0)

    def fill_chunk(c, carry):
        k0 = pl.multiple_of(c * kc, kc)
        o_ref[:, pl.ds(k0, kc)] = jnp.full((tq, kc), NEG, o_ref.dtype)
        return carry

    lax.fori_loop(n_chunks, s_total // kc, fill_chunk, 0)


def _indexer_mask(proj, iw, *, q_col, e1_col, e2_col, topk, tq=IDX_TQ, kc=IDX_KC):
    s = proj.shape[0]
    assert topk <= _CAND * LANE
    kern = functools.partial(_indexer_kernel, tq=tq, kc=kc, topk=topk,
                             wscale=(IDX_HEADS * IDX_DIM) ** -0.5)
    return pl.pallas_call(
        kern,
        grid=(s // tq,),
        in_specs=[
            pl.BlockSpec((tq, IDX_HEADS * IDX_DIM), lambda i: (i, q_col)),
            pl.BlockSpec((s, LANE), lambda i: (0, e1_col)),
            pl.BlockSpec((s, LANE), lambda i: (0, e2_col)),
            pl.BlockSpec((tq, LANE), lambda i: (i, 0)),
        ],
        out_specs=pl.BlockSpec((tq, s), lambda i: (i, 0)),
        out_shape=jax.ShapeDtypeStruct((s, s), BF16),
        scratch_shapes=[pltpu.VMEM((tq, -(-s // (_CAND * LANE)) * _CAND * LANE), F32),
                        pltpu.VMEM((tq, _CAND * LANE), jnp.int32),
                        pltpu.VMEM((tq, LANE), jnp.int32),
                        pltpu.VMEM((tq, LANE), jnp.int32)],
        compiler_params=_cparams(("parallel",)),
        name="indexer_topk",
    )(proj, proj, proj, iw)


def _rms_rows(x, g, eps):
    return x * lax.rsqrt(jnp.mean(x * x, axis=-1, keepdims=True) + eps) * g


def _mla_q_kernel(cq_ref, g_ref, wm_ref, wr_ref, cos_ref, sin_ref, o_ref, *, scale):
    n = _rms_rows(cq_ref[...], g_ref[...], RMS_EPS).astype(BF16)
    main = jnp.dot(n, wm_ref[...], preferred_element_type=F32)
    rot = jnp.dot(n, wr_ref[...], preferred_element_type=F32)
    cos, sin = cos_ref[...], sin_ref[...]
    for h in range(N_HEADS):
        base = h * 2 * LANE
        o_ref[:, base:base + LANE] = (main[:, base:base + LANE] * scale).astype(o_ref.dtype)
        pe = main[:, base + LANE:base + 2 * LANE] * cos + rot[:, h * LANE:(h + 1) * LANE] * sin
        o_ref[:, base + LANE:base + 2 * LANE] = (pe * scale).astype(o_ref.dtype)


def _mla_kv_kernel(ckv_ref, ef_ref, g_ref, w_ref, cos_ref, sin_ref, k_ref, v_ref):
    n = _rms_rows(ckv_ref[...], g_ref[...], RMS_EPS).astype(BF16)
    kv = jnp.dot(n, w_ref[...], preferred_element_type=F32)
    ef = ef_ref[...]
    kpe = (ef[:, :LANE] * cos_ref[...] + ef[:, LANE:] * sin_ref[...]).astype(k_ref.dtype)
    for h in range(N_HEADS):
        base = h * 2 * LANE
        k_ref[:, base:base + LANE] = kv[:, h * LANE:(h + 1) * LANE].astype(k_ref.dtype)
        k_ref[:, base + LANE:base + 2 * LANE] = kpe
    v_ref[...] = kv[:, HW:].astype(v_ref.dtype)


def _mla_prep(aux, gq, gkv, wq_main, wq_rot, wkv, cos_t, sin_t, *, tm=ROW_TM):
    s = aux.shape[0]
    scale = (NOPE_DIM + ROPE_DIM) ** -0.5 * LOG2E
    q = pl.pallas_call(
        functools.partial(_mla_q_kernel, scale=scale),
        grid=(s // tm,),
        in_specs=[
            pl.BlockSpec((tm, Q_LORA), lambda i: (i, 1)),
            pl.BlockSpec((1, Q_LORA), lambda i: (0, 0)),
            pl.BlockSpec(wq_main.shape, lambda i: (0, 0)),
            pl.BlockSpec(wq_rot.shape, lambda i: (0, 0)),
            pl.BlockSpec((tm, LANE), lambda i: (i, 0)),
            pl.BlockSpec((tm, LANE), lambda i: (i, 0)),
        ],
        out_specs=pl.BlockSpec((tm, 2 * HW), lambda i: (i, 0)),
        out_shape=jax.ShapeDtypeStruct((s, 2 * HW), BF16),
        compiler_params=_cparams(("parallel",)),
        name="mla_q_prep",
    )(aux, gq.reshape(1, Q_LORA), wq_main, wq_rot, cos_t, sin_t)
    k, v = pl.pallas_call(
        _mla_kv_kernel,
        grid=(s // tm,),
        in_specs=[
            pl.BlockSpec((tm, KV_LORA), lambda i: (i, 0)),
            pl.BlockSpec((tm, 2 * LANE), lambda i: (i, 2)),
            pl.BlockSpec((1, KV_LORA), lambda i: (0, 0)),
            pl.BlockSpec(wkv.shape, lambda i: (0, 0)),
            pl.BlockSpec((tm, LANE), lambda i: (i, 0)),
            pl.BlockSpec((tm, LANE), lambda i: (i, 0)),
        ],
        out_specs=[pl.BlockSpec((tm, 2 * HW), lambda i: (i, 0)), pl.BlockSpec((tm, HW), lambda i: (i, 0))],
        out_shape=[jax.ShapeDtypeStruct((s, 2 * HW), BF16), jax.ShapeDtypeStruct((s, HW), BF16)],
        compiler_params=_cparams(("parallel",)),
        name="mla_kv_prep",
    )(aux, aux, gkv.reshape(1, KV_LORA), wkv, cos_t, sin_t)
    return q, k, v


def _rope_tables(s):
    half = ROPE_DIM // 2
    inv = ROPE_THETA ** (-jnp.arange(half, dtype=F32) / half)
    ang = jnp.arange(s, dtype=jnp.int32).astype(F32)[:, None] * inv[None, :]
    cos, sin = jnp.cos(ang), jnp.sin(ang)
    pad = jnp.zeros((s, LANE - ROPE_DIM), F32)
    return (jnp.concatenate([cos, cos, pad], -1), jnp.concatenate([-sin, sin, pad], -1))


def _swap_halves(w):
    half = w.shape[-1] // 2
    return jnp.concatenate([w[..., half:], w[..., :half]], -1)


def _even_mixer(y, yb, w_in, w_out, g, b):
    s = y.shape[0]
    d = w_in.shape[0]
    att_scale = HEAD_DIM ** -0.5 * LOG2E
    n_qkv = 6 * HW
    iq_w = w_in[:, n_qkv:n_qkv + IDX_HEADS * IDX_DIM]
    ik_w = w_in[:, n_qkv + IDX_HEADS * IDX_DIM:n_qkv + IDX_HEADS * IDX_DIM + IDX_DIM]
    iw_w = w_in[:, n_qkv + IDX_HEADS * IDX_DIM + IDX_DIM:]
    z64 = jnp.zeros((d, LANE - IDX_DIM), F32)
    w_main = jnp.concatenate([w_in[:, :n_qkv], iq_w, ik_w, z64, z64, ik_w, jnp.zeros((d, 2 * LANE), F32)],
                             -1).astype(BF16)
    ones = jnp.ones((HW,), F32)
    colscale = jnp.concatenate([ones * att_scale, ones, ones, ones * att_scale, ones, ones, ones,
                                jnp.ones((4 * LANE,), F32)])
    proj = _matmul(yb, w_main, colscale, BF16, name="even_in_proj")
    w_iw = jnp.concatenate([iw_w, jnp.zeros((d, LANE - IDX_HEADS), F32)], -1).astype(BF16)
    iw = _matmul(yb, w_iw, jnp.ones((LANE,), F32), F32, tn=LANE, name="even_iw_proj")

    slopes = _alibi_slopes(N_HEADS)
    t = DIL_T
    table = jnp.asarray(_dilated_bias_table(t))
    lookback = max(w for w, _ in A_PATTERNS) // t
    oa = _attention(proj, proj, proj, q_col=0, k_col=1, v_col=2, dq=HEAD_DIM, slopes=slopes,
                    bias=table, bias_kind="table", lookback=lookback, tq=t, tk=t, name="dilated_attn")

    topk = min(B_TOPK_MAX, s // 4)
    mask = _indexer_mask(proj, iw, q_col=6, e1_col=7 * HW // LANE, e2_col=7 * HW // LANE + 1, topk=topk)
    ob = _attention(proj, proj, proj, q_col=3, k_col=4, v_col=5, dq=HEAD_DIM, slopes=slopes,
                    bias=mask, bias_kind="mask", tq=ATT_TQ, tk=ATT_TK, n_sub=ATT_NSUB, name="dsa_attn")
    return _proj_ln(oa, ob, w_out.astype(BF16), y, g, b)


def _odd_mixer(y, yb, w_in, w_out, lam_params, subln_g, gq, gkv, w_uq, w_ukv, lam_init, g, b):
    s = y.shape[0]
    d = w_in.shape[0]
    w_c = w_in[:, :3 * HW].astype(BF16)
    c_scale = jnp.concatenate([jnp.full((HW,), C_QK_DIM ** -0.5 * LOG2E, F32), jnp.ones((2 * HW,), F32)])
    proj = _matmul(yb, w_c, c_scale, BF16, name="odd_in_proj")
    mq_w = w_in[:, 3 * HW:3 * HW + Q_LORA]
    mkv_w = w_in[:, 3 * HW + Q_LORA:3 * HW + Q_LORA + KV_LORA]
    kpe_w = w_in[:, 3 * HW + Q_LORA + KV_LORA:]
    z64 = jnp.zeros((d, LANE - ROPE_DIM), F32)
    w_aux = jnp.concatenate([mkv_w, kpe_w, z64, _swap_halves(kpe_w), z64, mq_w], -1).astype(BF16)
    aux = _matmul(yb, w_aux, jnp.ones((w_aux.shape[1],), F32), F32, name="odd_aux_proj")

    wq = w_uq.reshape(Q_LORA, N_HEADS, NOPE_DIM + ROPE_DIM)
    zq = jnp.zeros((Q_LORA, N_HEADS, LANE - ROPE_DIM), F32)
    wq_main = jnp.concatenate([wq, zq], -1).reshape(Q_LORA, 2 * HW).astype(BF16)
    wq_rot = jnp.concatenate([_swap_halves(wq[..., NOPE_DIM:]), zq], -1).reshape(Q_LORA, HW).astype(BF16)
    wkv = w_ukv.reshape(KV_LORA, N_HEADS, NOPE_DIM + V_DIM)
    wkv = jnp.concatenate([wkv[..., :NOPE_DIM].reshape(KV_LORA, HW),
                           wkv[..., NOPE_DIM:].reshape(KV_LORA, HW)], -1).astype(BF16)
    cos_t, sin_t = _rope_tables(s)
    mq, mk, mv = _mla_prep(aux, gq, gkv, wq_main, wq_rot, wkv, cos_t, sin_t)

    oc = _attention(proj, proj, proj, q_col=0, k_col=1, v_col=2, dq=HEAD_DIM,
                    slopes=_alibi_slopes(N_HEADS), diff=True, lam=lam_params, subg=subln_g,
                    lam_init=lam_init, tq=ATT_TQ, tk=ATT_TK, name="diff_attn")
    od = _attention(mq, mk, mv, q_col=0, k_col=0, v_col=0, dq=2 * LANE, tq=ATT_TQ, tk=ATT_TK,
                    n_sub=ATT_NSUB, name="mla_attn")
    return _proj_ln(oc, od, w_out.astype(BF16), y, g, b)


def kernel(x, ffn_w_gate, ffn_w_up, ffn_w_down, ln_g, ln_b, ev_w_in, ev_w_out, od_w_in, od_w_out,
           od_lambda, od_subln_g, od_q_norm_g, od_kv_norm_g, od_w_uq, od_w_ukv):
    batch, seq, d_model = x.shape
    xs = x.reshape(batch * seq, d_model)
    wg, wu, wd = (w.astype(BF16) for w in (ffn_w_gate, ffn_w_up, ffn_w_down))
    outs = []
    for bi in range(batch):
        y = xs if batch == 1 else xs[bi * seq:(bi + 1) * seq]
        yb = y.astype(BF16)
        for i in range(DEPTH):
            j = i // 2
            y, yb = _ffn_ln(y, yb, wg, wu, wd, ln_g[i, 0], ln_b[i, 0], layer=i, sub=0)
            if i % 2 == 0:
                y, yb = _even_mixer(y, yb, ev_w_in[j], ev_w_out[j], ln_g[i, 1], ln_b[i, 1])
            else:
                lam_init = 0.8 - 0.6 * math.exp(-0.3 * i)
                y, yb = _odd_mixer(y, yb, od_w_in[j], od_w_out[j], od_lambda[j], od_subln_g[j],
                                   od_q_norm_g[j], od_kv_norm_g[j], od_w_uq[j], od_w_ukv[j], lam_init,
                                   ln_g[i, 1], ln_b[i, 1])
            y, yb = _ffn_ln(y, yb, wg, wu, wd, ln_g[i, 2], ln_b[i, 2], layer=i, sub=1)
        outs.append(y)
    out = outs[0] if batch == 1 else jnp.concatenate(outs, 0)
    return out.reshape(batch, seq, d_model)
```

```python
import functools
import math

import numpy as np
import jax
import jax.numpy as jnp
from jax import lax
from jax.experimental import pallas as pl
from jax.experimental.pallas import tpu as pltpu

F32 = jnp.float32
BF16 = jnp.bfloat16

D_MODEL = 2048
DEPTH = 2
HEAD_DIM = 128
N_HEADS = 8
A_PATTERNS = ((128, 1), (512, 4), (2048, 16))
IDX_HEADS = 16
IDX_DIM = 64
B_TOPK_MAX = 256
C_QK_DIM = 64
Q_LORA = 768
KV_LORA = 512
NOPE_DIM = 128
ROPE_DIM = 64
V_DIM = 128
ROPE_THETA = 10000.0
D_FF = 5632
LN_EPS = 1e-5
RMS_EPS = 1e-6
DIFF_EPS = 1e-5
ALPHA = (2 * DEPTH) ** 0.25

LANE = 128
NEG = -1e30
LOG2E = math.log2(math.e)
VMEM_LIMIT = 56 * 1024 * 1024

HW = N_HEADS * HEAD_DIM

ATT_TQ = 1024
ATT_TK = 512
ATT_NSUB = 2
DIL_T = 512
FFN_TM, FFN_TF = 512, 512
PROJ_TM, PROJ_TN = 1024, 512
ROW_TM = 512
IDX_TQ, IDX_KC = 256, 512
SEL_ROWS = 16


def _cparams(sem):
    return pltpu.CompilerParams(dimension_semantics=sem, vmem_limit_bytes=VMEM_LIMIT)


def _layernorm_rows(z, g, b):
    mu = jnp.mean(z, axis=-1, keepdims=True)
    zc = z - mu
    var = jnp.mean(zc * zc, axis=-1, keepdims=True)
    return zc * lax.rsqrt(var + LN_EPS) * g + b


def _ffn_ln_kernel(xb_ref, x_ref, wg_ref, wu_ref, wd_ref, g_ref, b_ref, y_ref, yb_ref, acc_sc):
    j = pl.program_id(1)

    @pl.when(j == 0)
    def _():
        acc_sc[...] = jnp.zeros_like(acc_sc)

    xb = xb_ref[...]
    gate = jnp.dot(xb, wg_ref[...], preferred_element_type=F32)
    up = jnp.dot(xb, wu_ref[...], preferred_element_type=F32)
    h = (gate * jax.nn.sigmoid(gate)) * up
    acc_sc[...] += jnp.dot(h.astype(BF16), wd_ref[...], preferred_element_type=F32)

    @pl.when(j == pl.num_programs(1) - 1)
    def _():
        z = ALPHA * x_ref[...] + 0.5 * acc_sc[...]
        y = _layernorm_rows(z, g_ref[...], b_ref[...])
        y_ref[...] = y
        yb_ref[...] = y.astype(BF16)


def _ffn_ln(x, xb, wg, wu, wd, g, b, *, layer, sub, tm=FFN_TM, tf=FFN_TF):
    s, d = x.shape
    dff = wg.shape[-1]
    row = lambda i, j: (i, 0)
    return pl.pallas_call(
        _ffn_ln_kernel,
        grid=(s // tm, dff // tf),
        in_specs=[
            pl.BlockSpec((tm, d), row),
            pl.BlockSpec((tm, d), row),
            pl.BlockSpec((None, None, d, tf), lambda i, j: (layer, sub, 0, j)),
            pl.BlockSpec((None, None, d, tf), lambda i, j: (layer, sub, 0, j)),
            pl.BlockSpec((None, None, tf, d), lambda i, j: (layer, sub, j, 0)),
            pl.BlockSpec((1, d), lambda i, j: (0, 0)),
            pl.BlockSpec((1, d), lambda i, j: (0, 0)),
        ],
        out_specs=[pl.BlockSpec((tm, d), row), pl.BlockSpec((tm, d), row)],
        out_shape=[jax.ShapeDtypeStruct((s, d), F32), jax.ShapeDtypeStruct((s, d), BF16)],
        scratch_shapes=[pltpu.VMEM((tm, d), F32)],
        compiler_params=_cparams(("parallel", "arbitrary")),
        name="ffn_ln",
    )(xb, x, wg, wu, wd, g.reshape(1, d), b.reshape(1, d))


def _mm_kernel(x_ref, w_ref, s_ref, o_ref):
    acc = jnp.dot(x_ref[...], w_ref[...], preferred_element_type=F32)
    o_ref[...] = (acc * s_ref[...]).astype(o_ref.dtype)


def _matmul(x, w, colscale, out_dtype, *, tm=PROJ_TM, tn=PROJ_TN, name="proj"):
    m, k = x.shape
    n = w.shape[1]
    tm, tn = min(tm, m), min(tn, n)
    assert m % tm == 0 and n % tn == 0
    return pl.pallas_call(
        _mm_kernel,
        grid=(m // tm, n // tn),
        in_specs=[
            pl.BlockSpec((tm, k), lambda i, j: (i, 0)),
            pl.BlockSpec((k, tn), lambda i, j: (0, j)),
            pl.BlockSpec((1, tn), lambda i, j: (0, j)),
        ],
        out_specs=pl.BlockSpec((tm, tn), lambda i, j: (i, j)),
        out_shape=jax.ShapeDtypeStruct((m, n), out_dtype),
        compiler_params=_cparams(("parallel", "arbitrary")),
        name=name,
    )(x, w, colscale.reshape(1, n))


def _proj_ln_kernel(oa_ref, ob_ref, wa_ref, wb_ref, x_ref, g_ref, b_ref, y_ref, yb_ref):
    mix = jnp.dot(oa_ref[...], wa_ref[...], preferred_element_type=F32)
    mix += jnp.dot(ob_ref[...], wb_ref[...], preferred_element_type=F32)
    y = _layernorm_rows(ALPHA * x_ref[...] + mix, g_ref[...], b_ref[...])
    y_ref[...] = y
    yb_ref[...] = y.astype(BF16)


def _proj_ln(oa, ob, w, x, g, b, *, tm=ROW_TM):
    s, d = x.shape
    ka, kb = oa.shape[1], ob.shape[1]
    assert ka == kb and w.shape[0] == ka + kb
    return pl.pallas_call(
        _proj_ln_kernel,
        grid=(s // tm,),
        in_specs=[
            pl.BlockSpec((tm, ka), lambda i: (i, 0)),
            pl.BlockSpec((tm, kb), lambda i: (i, 0)),
            pl.BlockSpec((ka, d), lambda i: (0, 0)),
            pl.BlockSpec((kb, d), lambda i: (1, 0)),
            pl.BlockSpec((tm, d), lambda i: (i, 0)),
            pl.BlockSpec((1, d), lambda i: (0, 0)),
            pl.BlockSpec((1, d), lambda i: (0, 0)),
        ],
        out_specs=[
            pl.BlockSpec((tm, d), lambda i: (i, 0)),
            pl.BlockSpec((tm, d), lambda i: (i, 0)),
        ],
        out_shape=[jax.ShapeDtypeStruct((s, d), F32), jax.ShapeDtypeStruct((s, d), BF16)],
        compiler_params=_cparams(("parallel",)),
        name="proj_ln",
    )(oa, ob, w, w, x, g.reshape(1, d), b.reshape(1, d))


_FIRST, _LAST, _DIAG = 1, 2, 4


def _attn_kernel(qi_ref, ki_ref, bi_ref, fl_ref, *refs, tq, tk, n_sub, dq, slopes, bias_kind, diff,
                 lam_init):
    refs = list(refs)
    q_ref, k_ref, v_ref = refs[:3]
    pos = 3
    bias_ref = None
    if bias_kind is not None:
        bias_ref = refs[pos]
        pos += 1
    if diff:
        lam_ref, subg_ref = refs[pos], refs[pos + 1]
        pos += 2
    o_ref = refs[pos]
    m_sc, l_sc, acc_sc = refs[pos + 1:pos + 4]

    p = pl.program_id(0)
    fl = fl_ref[p]
    q0 = qi_ref[p] * tq
    k0 = ki_ref[p] * (tk * n_sub)
    rows = 2 * tq if diff else tq

    @pl.when((fl & _FIRST) != 0)
    def _():
        m_sc[...] = jnp.full_like(m_sc, NEG)
        l_sc[...] = jnp.zeros_like(l_sc)
        acc_sc[...] = jnp.zeros_like(acc_sc)

    def step(masked):
        for sub in range(n_sub):
            sub_step(masked, sub * tk)

    def sub_step(masked, off):
        kq = k0 + off - q0
        if slopes is not None:
            kp = kq.astype(F32) + lax.broadcasted_iota(jnp.int32, (1, tk), 1).astype(F32)
        if bias_ref is not None:
            if bias_kind == "table":
                bias = bias_ref[...].reshape(tq, tk)
            else:
                bias = bias_ref[:, off:off + tk].astype(F32)
        if masked:
            r = lax.broadcasted_iota(jnp.int32, (rows, tk), 0)
            if diff:
                r = jnp.where(r >= tq, r - tq, r)
            c = lax.broadcasted_iota(jnp.int32, (rows, tk), 1)
            keep = (r - c) >= kq
        ones = jnp.ones((tk, LANE), BF16)
        for h in range(N_HEADS):
            if diff:
                qh = q_ref[:, h * HEAD_DIM:(h + 1) * HEAD_DIM]
                lane = lax.broadcasted_iota(jnp.int32, (tq, HEAD_DIM), 1)
                zero = jnp.zeros_like(qh)
                q = jnp.concatenate([jnp.where(lane < C_QK_DIM, qh, zero),
                                     jnp.where(lane >= C_QK_DIM, qh, zero)], axis=0)
            else:
                q = q_ref[:, h * dq:(h + 1) * dq]
            k = k_ref[off:off + tk, h * dq:(h + 1) * dq]
            s = lax.dot_general(q, k, (((1,), (1,)), ((), ())), preferred_element_type=F32)
            if slopes is not None:
                s = s + (slopes[h] * LOG2E) * kp
            if bias_ref is not None:
                s = s + bias
            if masked:
                s = jnp.where(keep, s, NEG)
            m_prev = m_sc[h]
            m_new = jnp.maximum(m_prev, jnp.max(s, axis=-1, keepdims=True))
            alpha = jnp.exp2(m_prev - m_new)
            pexp = jnp.exp2(s - jnp.concatenate([m_new] * (tk // LANE), axis=-1))
            v_aug = jnp.concatenate([v_ref[off:off + tk, h * HEAD_DIM:(h + 1) * HEAD_DIM], ones], axis=-1)
            pv = jnp.dot(pexp.astype(BF16), v_aug, preferred_element_type=F32)
            l_sc[h] = alpha * l_sc[h] + pv[:, HEAD_DIM:]
            acc_sc[h] = alpha * acc_sc[h] + pv[:, :HEAD_DIM]
            m_sc[h] = m_new

    if bias_kind is None:
        @pl.when((fl & _DIAG) != 0)
        def _():
            step(True)

        @pl.when((fl & _DIAG) == 0)
        def _():
            step(False)
    else:
        step(False)

    @pl.when((fl & _LAST) != 0)
    def _():
        if diff:
            lp = lam_ref[...]
            lam = (jnp.exp(jnp.sum(lp[0:1] * lp[1:2], axis=-1, keepdims=True))
                   - jnp.exp(jnp.sum(lp[2:3] * lp[3:4], axis=-1, keepdims=True)) + lam_init)
            for h in range(N_HEADS):
                o1 = acc_sc[h, :tq, :] / l_sc[h, :tq, :]
                o2 = acc_sc[h, tq:, :] / l_sc[h, tq:, :]
                o = o1 - lam * o2
                ms = jnp.mean(o * o, axis=-1, keepdims=True)
                o = o * lax.rsqrt(ms + DIFF_EPS) * subg_ref[...]
                o_ref[:, h * HEAD_DIM:(h + 1) * HEAD_DIM] = (o * (1.0 - lam_init)).astype(o_ref.dtype)
        else:
            for h in range(N_HEADS):
                o_ref[:, h * HEAD_DIM:(h + 1) * HEAD_DIM] = (acc_sc[h] / l_sc[h]).astype(o_ref.dtype)


def _causal_pairs(nq, ratio=1, lookback=None):
    qi, ki, bi, fl = [], [], [], []
    for q in range(nq):
        lo = 0 if lookback is None else max(0, q * ratio - lookback)
        hi = (q + 1) * ratio - 1
        for k in range(lo, hi + 1):
            qi.append(q)
            ki.append(k)
            bi.append(q * ratio - k)
            fl.append((_FIRST if k == lo else 0) | (_LAST if k == hi else 0)
                      | (_DIAG if k >= q * ratio else 0))
    return tuple(np.asarray(a, np.int32) for a in (qi, ki, bi, fl))


def _attention(q_arr, k_arr, v_arr, *, q_col, k_col, v_col, dq, slopes=None, bias=None, bias_kind=None,
               lookback=None, diff=False, lam=None, subg=None, lam_init=0.0, tq=512, tk=512, n_sub=1,
               name="attn"):
    s = q_arr.shape[0]
    tq = min(tq, s)
    tkb = tk * n_sub
    assert tq % tkb == 0 and s % tq == 0 and (bias_kind != "table" or (tq == tk and n_sub == 1))
    nq = s // tq
    qi, ki, bi, fl = _causal_pairs(nq, tq // tkb, lookback)
    qw = N_HEADS * (HEAD_DIM if diff else dq)
    kw = N_HEADS * dq
    in_specs = [
        pl.BlockSpec((tq, qw), lambda p, qi, ki, bi, fl: (qi[p], q_col)),
        pl.BlockSpec((tkb, kw), lambda p, qi, ki, bi, fl: (ki[p], k_col)),
        pl.BlockSpec((tkb, HW), lambda p, qi, ki, bi, fl: (ki[p], v_col)),
    ]
    args = [q_arr, k_arr, v_arr]
    if bias_kind == "table":
        in_specs.append(pl.BlockSpec((1, tq, tk), lambda p, qi, ki, bi, fl: (bi[p], 0, 0)))
        args.append(bias)
    elif bias_kind == "mask":
        in_specs.append(pl.BlockSpec((tq, tkb), lambda p, qi, ki, bi, fl: (qi[p], ki[p])))
        args.append(bias)
    rows = 2 * tq if diff else tq
    scratch = [pltpu.VMEM((N_HEADS, rows, LANE), F32), pltpu.VMEM((N_HEADS, rows, LANE), F32),
               pltpu.VMEM((N_HEADS, rows, HEAD_DIM), F32)]
    if diff:
        in_specs.append(pl.BlockSpec((4, C_QK_DIM), lambda p, qi, ki, bi, fl: (0, 0)))
        in_specs.append(pl.BlockSpec((1, HEAD_DIM), lambda p, qi, ki, bi, fl: (0, 0)))
        args += [lam, subg.reshape(1, HEAD_DIM)]
    kern = functools.partial(_attn_kernel, tq=tq, tk=tk, n_sub=n_sub, dq=dq, slopes=slopes,
                             bias_kind=bias_kind, diff=diff, lam_init=lam_init)
    return pl.pallas_call(
        kern,
        grid_spec=pltpu.PrefetchScalarGridSpec(
            num_scalar_prefetch=4,
            grid=(len(qi),),
            in_specs=in_specs,
            out_specs=pl.BlockSpec((tq, HW), lambda p, qi, ki, bi, fl: (qi[p], 0)),
            scratch_shapes=scratch,
        ),
        out_shape=jax.ShapeDtypeStruct((s, HW), BF16),
        compiler_params=_cparams(("arbitrary",)),
        name=name,
    )(jnp.asarray(qi), jnp.asarray(ki), jnp.asarray(bi), jnp.asarray(fl), *args)


def _alibi_slopes(n):
    return tuple(2.0 ** (-8.0 * (h + 1) / n) for h in range(n))


def _dilated_bias_table(t):
    max_win = max(w for w, _ in A_PATTERNS)
    n_off = max_win // t + 1
    r = np.arange(t)[:, None]
    c = np.arange(t)[None, :]
    out = np.empty((n_off, t, t), np.float32)
    for o in range(n_off):
        dist = o * t + r - c
        mult = np.zeros((t, t), np.float64)
        for win, dil in A_PATTERNS:
            mult += (dist >= 0) & (dist <= win) & (dist % dil == 0)
        out[o] = np.where(mult > 0, np.log2(np.maximum(mult, 1.0)), NEG)
    return out


def _sortable_key(x):
    bits = pltpu.bitcast(x, jnp.int32)
    return jnp.where(bits < 0, bits ^ jnp.int32(0x7FFFFFFF), bits)


_CAND = 16


def _bitonic_sort_desc(x):
    x = list(x)
    n = len(x)
    k = 2
    while k <= n:
        j = k // 2
        while j >= 1:
            for i in range(n):
                l = i ^ j
                if l > i:
                    hi, lo = jnp.maximum(x[i], x[l]), jnp.minimum(x[i], x[l])
                    x[i], x[l] = (hi, lo) if (i & k) == 0 else (lo, hi)
            j //= 2
        k *= 2
    return x


def _merge_top_desc(a, b):
    n = len(a)
    c = [jnp.maximum(a[i], b[n - 1 - i]) for i in range(n)]
    j = n // 2
    while j >= 1:
        for i in range(n):
            l = i ^ j
            if l > i:
                c[i], c[l] = jnp.maximum(c[i], c[l]), jnp.minimum(c[i], c[l])
        j //= 2
    return c


def _indexer_kernel(q_ref, e1_ref, e2_ref, w_ref, o_ref, key_sc, cand_sc, thr_sc, nge_sc, *, tq, kc, topk,
                    wscale):
    qb = pl.program_id(0)
    q0 = qb * tq
    s_total = o_ref.shape[1]
    n_chunks = (q0 + tq + kc - 1) // kc
    w = w_ref[...] * wscale
    row = lax.broadcasted_iota(jnp.int32, (tq, kc), 0)
    col = lax.broadcasted_iota(jnp.int32, (tq, kc), 1)
    rel = row - col

    def score_chunk(c, carry):
        k0 = pl.multiple_of(c * kc, kc)
        k1 = e1_ref[pl.ds(k0, kc), :]
        k2 = e2_ref[pl.ds(k0, kc), :]
        acc = jnp.zeros((tq, kc), F32)
        for j in range(IDX_HEADS // 2):
            qp = q_ref[:, j * LANE:(j + 1) * LANE]
            s_even = lax.dot_general(qp, k1, (((1,), (1,)), ((), ())), preferred_element_type=F32)
            s_odd = lax.dot_general(qp, k2, (((1,), (1,)), ((), ())), preferred_element_type=F32)
            acc = acc + jnp.maximum(s_even, 0.0) * w[:, 2 * j:2 * j + 1]
            acc = acc + jnp.maximum(s_odd, 0.0) * w[:, 2 * j + 1:2 * j + 2]
        acc = jnp.where(acc == 0.0, 0.0, acc)
        key_sc[:, pl.ds(k0, kc)] = jnp.where(rel >= k0 - q0, acc, -jnp.inf)
        return carry

    lax.fori_loop(0, n_chunks, score_chunk, 0)

    int_min = jnp.int32(-2 ** 31)
    n_tiles = n_chunks * (kc // LANE)
    n_groups = (n_tiles + _CAND - 1) // _CAND

    def pad_tile(t, carry):
        key_sc[:, pl.ds(pl.multiple_of(t * LANE, LANE), LANE)] = jnp.full((tq, LANE), -jnp.inf, F32)
        return carry

    lax.fori_loop(n_tiles, n_groups * _CAND, pad_tile, 0)

    def row_group(r, carry):
        r0 = pl.multiple_of(r * SEL_ROWS, SEL_ROWS)

        def tile_group(g, top):
            base = g * (_CAND * LANE)
            x = [key_sc[pl.ds(r0, SEL_ROWS), pl.ds(pl.multiple_of(base + j * LANE, LANE), LANE)]
                 for j in range(_CAND)]
            return tuple(_merge_top_desc(list(top), _bitonic_sort_desc(x)))

        top0 = tuple(jnp.full((SEL_ROWS, LANE), -jnp.inf, F32) for _ in range(_CAND))
        top = lax.fori_loop(0, n_groups, tile_group, top0)
        for j in range(_CAND):
            cand_sc[pl.ds(r0, SEL_ROWS), j * LANE:(j + 1) * LANE] = _sortable_key(top[j])
        return carry

    lax.fori_loop(0, tq // SEL_ROWS, row_group, 0)

    def count_cand(cand):
        cand_b = jnp.broadcast_to(cand, (tq, LANE))
        cnt = jnp.zeros((tq, LANE), jnp.int32)
        for j in range(_CAND):
            cnt = cnt + jnp.where(cand_sc[:, j * LANE:(j + 1) * LANE] >= cand_b, 1, 0)
        return jnp.sum(cnt, axis=-1, keepdims=True)

    def count_all(cand):
        cand_b = jnp.broadcast_to(cand, (tq, LANE))

        def body(t, cnt):
            blk = _sortable_key(key_sc[:, pl.ds(pl.multiple_of(t * LANE, LANE), LANE)])
            return cnt + jnp.where(blk >= cand_b, 1, 0)

        cnt = lax.fori_loop(0, n_tiles, body, jnp.zeros((tq, LANE), jnp.int32))
        return jnp.sum(cnt, axis=-1, keepdims=True)

    def kth_largest(count_ge, lo, hi):
        n_bits = jnp.max(32 - lax.clz(lo ^ hi))
        keep = jnp.where(n_bits >= 32, 0, -lax.shift_left(jnp.int32(1), jnp.minimum(n_bits, 31)))
        floor = jnp.where(n_bits >= 32, int_min, lo & keep)

        def bit_step(i, thr):
            cand = thr + lax.shift_left(jnp.int32(1), n_bits - 1 - i)
            return jnp.where(count_ge(cand) >= topk, cand, thr)

        return lax.fori_loop(0, n_bits, bit_step, floor)

    lo_tile = (topk + LANE - 1) // LANE - 1
    hi_tile = (topk - 1) // LANE
    lo = jnp.min(cand_sc[:, lo_tile * LANE:(lo_tile + 1) * LANE], axis=-1, keepdims=True)
    hi = jnp.max(cand_sc[:, hi_tile * LANE:(hi_tile + 1) * LANE], axis=-1, keepdims=True)
    thr = kth_largest(count_cand, lo, hi)
    thr_sc[...] = jnp.broadcast_to(thr, (tq, LANE))
    nge_sc[...] = jnp.broadcast_to(count_cand(thr), (tq, LANE))
    smallest_kept = cand_sc[:, (_CAND - 1) * LANE:]
    unsafe = jnp.max(jnp.where(smallest_kept >= thr_sc[...], 1, 0)) > 0

    @pl.when(jnp.logical_and(n_groups > 1, unsafe))
    def _():
        everything = (jnp.full((tq, 1), int_min, jnp.int32), jnp.full((tq, 1), 2 ** 31 - 1, jnp.int32))
        thr_all = kth_largest(count_all, *everything)
        thr_sc[...] = jnp.broadcast_to(thr_all, (tq, LANE))
        nge_sc[...] = jnp.broadcast_to(count_all(thr_all), (tq, LANE))

    thr_key = thr_sc[...]
    thr_b = pltpu.bitcast(jnp.where(thr_key < 0, thr_key ^ jnp.int32(0x7FFFFFFF), thr_key), F32)
    row_t = lax.broadcasted_iota(jnp.int32, (tq, LANE), 0)
    col_t = lax.broadcasted_iota(jnp.int32, (tq, LANE), 1)
    rel_t = row_t - col_t
    tied = jnp.max(jnp.where(nge_sc[...] > topk, 1, 0)) > 0

    @pl.when(jnp.logical_not(tied))
    def _():
        def mask_tile(t, carry):
            k0 = pl.multiple_of(t * LANE, LANE)
            sel = (key_sc[:, pl.ds(k0, LANE)] >= thr_b) & (rel_t >= k0 - q0)
            o_ref[:, pl.ds(k0, LANE)] = jnp.where(sel, 0.0, NEG).astype(o_ref.dtype)
            return carry

        lax.fori_loop(0, n_tiles, mask_tile, 0)

    @pl.when(tied)
    def _():
        need = topk - count_all(thr_key[:, :1] + 1)

        def count_tied_upto(pos_max):
            pos_b = jnp.broadcast_to(pos_max, (tq, LANE))

            def body(t, cnt):
                k0 = pl.multiple_of(t * LANE, LANE)
                blk = _sortable_key(key_sc[:, pl.ds(k0, LANE)])
                return cnt + jnp.where((blk == thr_key) & (col_t + k0 <= pos_b), 1, 0)

            cnt = lax.fori_loop(0, n_tiles, body, jnp.zeros((tq, LANE), jnp.int32))
            return jnp.sum(cnt, axis=-1, keepdims=True)

        n_pos_bits = max(1, (key_sc.shape[1] - 1).bit_length())

        def bit_step(i, cut):
            cand = cut + lax.shift_left(jnp.int32(1), n_pos_bits - 1 - i)
            return jnp.where(count_tied_upto(cand - 1) < need, cand, cut)

        cut = lax.fori_loop(0, n_pos_bits, bit_step, jnp.zeros((tq, 1), jnp.int32))
        cut_b = jnp.broadcast_to(cut, (tq, LANE))

        def mask_tile(t, carry):
            k0 = pl.multiple_of(t * LANE, LANE)
            blk = _sortable_key(key_sc[:, pl.ds(k0, LANE)])
            sel = (blk > thr_key) | ((blk == thr_key) & (col_t + k0 <= cut_b))
            sel = sel & (rel_t >= k0 - q0)
            o_ref[:, pl.ds(k0, LANE)] = jnp.where(sel, 0.0, NEG).astype(o_ref.dtype)
            return carry

        lax.fori_loop(0, n_tiles, mask_tile, 0)

    def fill_chunk(c, carry):
        k0 = pl.multiple_of(c * kc, kc)
        o_ref[:, pl.ds(k0, kc)] = jnp.full((tq, kc), NEG, o_ref.dtype)
        return carry

    lax.fori_loop(n_chunks, s_total // kc, fill_chunk, 0)


def _indexer_mask(proj, iw, *, q_col, e1_col, e2_col, topk, tq=IDX_TQ, kc=IDX_KC):
    s = proj.shape[0]
    assert topk <= _CAND * LANE
    kern = functools.partial(_indexer_kernel, tq=tq, kc=kc, topk=topk,
                             wscale=(IDX_HEADS * IDX_DIM) ** -0.5)
    return pl.pallas_call(
        kern,
        grid=(s // tq,),
        in_specs=[
            pl.BlockSpec((tq, IDX_HEADS * IDX_DIM), lambda i: (i, q_col)),
            pl.BlockSpec((s, LANE), lambda i: (0, e1_col)),
            pl.BlockSpec((s, LANE), lambda i: (0, e2_col)),
            pl.BlockSpec((tq, LANE), lambda i: (i, 0)),
        ],
        out_specs=pl.BlockSpec((tq, s), lambda i: (i, 0)),
        out_shape=jax.ShapeDtypeStruct((s, s), BF16),
        scratch_shapes=[pltpu.VMEM((tq, -(-s // (_CAND * LANE)) * _CAND * LANE), F32),
                        pltpu.VMEM((tq, _CAND * LANE), jnp.int32),
                        pltpu.VMEM((tq, LANE), jnp.int32),
                        pltpu.VMEM((tq, LANE), jnp.int32)],
        compiler_params=_cparams(("parallel",)),
        name="indexer_topk",
    )(proj, proj, proj, iw)


def _rms_rows(x, g, eps):
    return x * lax.rsqrt(jnp.mean(x * x, axis=-1, keepdims=True) + eps) * g


def _mla_q_kernel(cq_ref, g_ref, wm_ref, wr_ref, cos_ref, sin_ref, o_ref, *, scale):
    n = _rms_rows(cq_ref[...], g_ref[...], RMS_EPS).astype(BF16)
    main = jnp.dot(n, wm_ref[...], preferred_element_type=F32)
    rot = jnp.dot(n, wr_ref[...], preferred_element_type=F32)
    cos, sin = cos_ref[...], sin_ref[...]
    for h in range(N_HEADS):
        base = h * 2 * LANE
        o_ref[:, base:base + LANE] = (main[:, base:base + LANE] * scale).astype(o_ref.dtype)
        pe = main[:, base + LANE:base + 2 * LANE] * cos + rot[:, h * LANE:(h + 1) * LANE] * sin
        o_ref[:, base + LANE:base + 2 * LANE] = (pe * scale).astype(o_ref.dtype)


def _mla_kv_kernel(ckv_ref, ef_ref, g_ref, w_ref, cos_ref, sin_ref, k_ref, v_ref):
    n = _rms_rows(ckv_ref[...], g_ref[...], RMS_EPS).astype(BF16)
    kv = jnp.dot(n, w_ref[...], preferred_element_type=F32)
    ef = ef_ref[...]
    kpe = (ef[:, :LANE] * cos_ref[...] + ef[:, LANE:] * sin_ref[...]).astype(k_ref.dtype)
    for h in range(N_HEADS):
        base = h * 2 * LANE
        k_ref[:, base:base + LANE] = kv[:, h * LANE:(h + 1) * LANE].astype(k_ref.dtype)
        k_ref[:, base + LANE:base + 2 * LANE] = kpe
    v_ref[...] = kv[:, HW:].astype(v_ref.dtype)


def _mla_prep(aux, gq, gkv, wq_main, wq_rot, wkv, cos_t, sin_t, *, tm=ROW_TM):
    s = aux.shape[0]
    scale = (NOPE_DIM + ROPE_DIM) ** -0.5 * LOG2E
    q = pl.pallas_call(
        functools.partial(_mla_q_kernel, scale=scale),
        grid=(s // tm,),
        in_specs=[
            pl.BlockSpec((tm, Q_LORA), lambda i: (i, 1)),
            pl.BlockSpec((1, Q_LORA), lambda i: (0, 0)),
            pl.BlockSpec(wq_main.shape, lambda i: (0, 0)),
            pl.BlockSpec(wq_rot.shape, lambda i: (0, 0)),
            pl.BlockSpec((tm, LANE), lambda i: (i, 0)),
            pl.BlockSpec((tm, LANE), lambda i: (i, 0)),
        ],
        out_specs=pl.BlockSpec((tm, 2 * HW), lambda i: (i, 0)),
        out_shape=jax.ShapeDtypeStruct((s, 2 * HW), BF16),
        compiler_params=_cparams(("parallel",)),
        name="mla_q_prep",
    )(aux, gq.reshape(1, Q_LORA), wq_main, wq_rot, cos_t, sin_t)
    k, v = pl.pallas_call(
        _mla_kv_kernel,
        grid=(s // tm,),
        in_specs=[
            pl.BlockSpec((tm, KV_LORA), lambda i: (i, 0)),
            pl.BlockSpec((tm, 2 * LANE), lambda i: (i, 2)),
            pl.BlockSpec((1, KV_LORA), lambda i: (0, 0)),
            pl.BlockSpec(wkv.shape, lambda i: (0, 0)),
            pl.BlockSpec((tm, LANE), lambda i: (i, 0)),
            pl.BlockSpec((tm, LANE), lambda i: (i, 0)),
        ],
        out_specs=[pl.BlockSpec((tm, 2 * HW), lambda i: (i, 0)), pl.BlockSpec((tm, HW), lambda i: (i, 0))],
        out_shape=[jax.ShapeDtypeStruct((s, 2 * HW), BF16), jax.ShapeDtypeStruct((s, HW), BF16)],
        compiler_params=_cparams(("parallel",)),
        name="mla_kv_prep",
    )(aux, aux, gkv.reshape(1, KV_LORA), wkv, cos_t, sin_t)
    return q, k, v


def _rope_tables(s):
    half = ROPE_DIM // 2
    inv = ROPE_THETA ** (-jnp.arange(half, dtype=F32) / half)
    ang = jnp.arange(s, dtype=jnp.int32).astype(F32)[:, None] * inv[None, :]
    cos, sin = jnp.cos(ang), jnp.sin(ang)
    pad = jnp.zeros((s, LANE - ROPE_DIM), F32)
    return (jnp.concatenate([cos, cos, pad], -1), jnp.concatenate([-sin, sin, pad], -1))


def _swap_halves(w):
    half = w.shape[-1] // 2
    return jnp.concatenate([w[..., half:], w[..., :half]], -1)


def _even_mixer(y, yb, w_in, w_out, g, b):
    s = y.shape[0]
    d = w_in.shape[0]
    att_scale = HEAD_DIM ** -0.5 * LOG2E
    n_qkv = 6 * HW
    iq_w = w_in[:, n_qkv:n_qkv + IDX_HEADS * IDX_DIM]
    ik_w = w_in[:, n_qkv + IDX_HEADS * IDX_DIM:n_qkv + IDX_HEADS * IDX_DIM + IDX_DIM]
    iw_w = w_in[:, n_qkv + IDX_HEADS * IDX_DIM + IDX_DIM:]
    z64 = jnp.zeros((d, LANE - IDX_DIM), F32)
    w_main = jnp.concatenate([w_in[:, :n_qkv], iq_w, ik_w, z64, z64, ik_w, jnp.zeros((d, 2 * LANE), F32)],
                             -1).astype(BF16)
    ones = jnp.ones((HW,), F32)
    colscale = jnp.concatenate([ones * att_scale, ones, ones, ones * att_scale, ones, ones, ones,
                                jnp.ones((4 * LANE,), F32)])
    proj = _matmul(yb, w_main, colscale, BF16, name="even_in_proj")
    w_iw = jnp.concatenate([iw_w, jnp.zeros((d, LANE - IDX_HEADS), F32)], -1).astype(BF16)
    iw = _matmul(yb, w_iw, jnp.ones((LANE,), F32), F32, tn=LANE, name="even_iw_proj")

    slopes = _alibi_slopes(N_HEADS)
    t = DIL_T
    table = jnp.asarray(_dilated_bias_table(t))
    lookback = max(w for w, _ in A_PATTERNS) // t
    oa = _attention(proj, proj, proj, q_col=0, k_col=1, v_col=2, dq=HEAD_DIM, slopes=slopes,
                    bias=table, bias_kind="table", lookback=lookback, tq=t, tk=t, name="dilated_attn")

    topk = min(B_TOPK_MAX, s // 4)
    mask = _indexer_mask(proj, iw, q_col=6, e1_col=7 * HW // LANE, e2_col=7 * HW // LANE + 1, topk=topk)
    ob = _attention(proj, proj, proj, q_col=3, k_col=4, v_col=5, dq=HEAD_DIM, slopes=slopes,
                    bias=mask, bias_kind="mask", tq=ATT_TQ, tk=ATT_TK, n_sub=ATT_NSUB, name="dsa_attn")
    return _proj_ln(oa, ob, w_out.astype(BF16), y, g, b)


def _odd_mixer(y, yb, w_in, w_out, lam_params, subln_g, gq, gkv, w_uq, w_ukv, lam_init, g, b):
    s = y.shape[0]
    d = w_in.shape[0]
    w_c = w_in[:, :3 * HW].astype(BF16)
    c_scale = jnp.concatenate([jnp.full((HW,), C_QK_DIM ** -0.5 * LOG2E, F32), jnp.ones((2 * HW,), F32)])
    proj = _matmul(yb, w_c, c_scale, BF16, name="odd_in_proj")
    mq_w = w_in[:, 3 * HW:3 * HW + Q_LORA]
    mkv_w = w_in[:, 3 * HW + Q_LORA:3 * HW + Q_LORA + KV_LORA]
    kpe_w = w_in[:, 3 * HW + Q_LORA + KV_LORA:]
    z64 = jnp.zeros((d, LANE - ROPE_DIM), F32)
    w_aux = jnp.concatenate([mkv_w, kpe_w, z64, _swap_halves(kpe_w), z64, mq_w], -1).astype(BF16)
    aux = _matmul(yb, w_aux, jnp.ones((w_aux.shape[1],), F32), F32, name="odd_aux_proj")

    wq = w_uq.reshape(Q_LORA, N_HEADS, NOPE_DIM + ROPE_DIM)
    zq = jnp.zeros((Q_LORA, N_HEADS, LANE - ROPE_DIM), F32)
    wq_main = jnp.concatenate([wq, zq], -1).reshape(Q_LORA, 2 * HW).astype(BF16)
    wq_rot = jnp.concatenate([_swap_halves(wq[..., NOPE_DIM:]), zq], -1).reshape(Q_LORA, HW).astype(BF16)
    wkv = w_ukv.reshape(KV_LORA, N_HEADS, NOPE_DIM + V_DIM)
    wkv = jnp.concatenate([wkv[..., :NOPE_DIM].reshape(KV_LORA, HW),
                           wkv[..., NOPE_DIM:].reshape(KV_LORA, HW)], -1).astype(BF16)
    cos_t, sin_t = _rope_tables(s)
    mq, mk, mv = _mla_prep(aux, gq, gkv, wq_main, wq_rot, wkv, cos_t, sin_t)

    oc = _attention(proj, proj, proj, q_col=0, k_col=1, v_col=2, dq=HEAD_DIM,
                    slopes=_alibi_slopes(N_HEADS), diff=True, lam=lam_params, subg=subln_g,
                    lam_init=lam_init, tq=ATT_TQ, tk=ATT_TK, n_sub=ATT_NSUB, name="diff_attn")
    od = _attention(mq, mk, mv, q_col=0, k_col=0, v_col=0, dq=2 * LANE, tq=ATT_TQ, tk=ATT_TK,
                    n_sub=ATT_NSUB, name="mla_attn")
    return _proj_ln(oc, od, w_out.astype(BF16), y, g, b)


def kernel(x, ffn_w_gate, ffn_w_up, ffn_w_down, ln_g, ln_b, ev_w_in, ev_w_out, od_w_in, od_w_out,
           od_lambda, od_subln_g, od_q_norm_g, od_kv_norm_g, od_w_uq, od_w_ukv):
    batch, seq, d_model = x.shape
    xs = x.reshape(batch * seq, d_model)
    wg, wu, wd = (w.astype(BF16) for w in (ffn_w_gate, ffn_w_up, ffn_w_down))
    outs = []
    for bi in range(batch):
        y = xs if batch == 1 else xs[bi * seq:(bi + 1) * seq]
        yb = y.astype(BF16)
        for i in range(DEPTH):
            j = i // 2
            y, yb = _ffn_ln(y, yb, wg, wu, wd, ln_g[i, 0], ln_b[i, 0], layer=i, sub=0)
            if i % 2 == 0:
                y, yb = _even_mixer(y, yb, ev_w_in[j], ev_w_out[j], ln_g[i, 1], ln_b[i, 1])
            else:
                lam_init = 0.8 - 0.6 * math.exp(-0.3 * i)
                y, yb = _odd_mixer(y, yb, od_w_in[j], od_w_out[j], od_lambda[j], od_subln_g[j],
                                   od_q_norm_g[j], od_kv_norm_g[j], od_w_uq[j], od_w_ukv[j], lam_init,
                                   ln_g[i, 1], ln_b[i, 1])
            y, yb = _ffn_ln(y, yb, wg, wu, wd, ln_g[i, 2], ln_b[i, 2], layer=i, sub=1)
        outs.append(y)
    out = outs[0] if batch == 1 else jnp.concatenate(outs, 0)
    return out.reshape(batch, seq, d_model)
```
